```python
import jax, jax.numpy as jnp
from jax import lax
import numpy as np

D_MODEL = 1024
BATCH = 8
SEQ = 2048
DEPTH = 4
DEC_BATCH = 128
DEC_SEQ = 1
PAST_LEN = 16384
PAGE_SIZE = 128

D_MIX = D_MODEL
D_A = D_MIX // 2
HGRN_EXPAND = 128
H_A = D_A // HGRN_EXPAND
DK_A = HGRN_EXPAND
DV_A = D_A // H_A
D_B = D_MIX - D_A
CONV_W = 31
D_IN = 4 * D_A + 2 * D_B
D_FF = ((8 * D_MODEL // 3 + 127) // 128) * 128
PLE_DIM = 256
CHUNK = 64
EPS = 1e-6

kernel_name = "hymba_hgrn2_conformer_macaron_step"


def rmsnorm(x, g):
    xf = x.astype(jnp.float32)
    y = xf * lax.rsqrt(jnp.mean(xf * xf, axis=-1, keepdims=True) + EPS)
    return (y * g.astype(jnp.float32)).astype(x.dtype)


def layernorm(x, g, b):
    xf = x.astype(jnp.float32)
    mu = jnp.mean(xf, axis=-1, keepdims=True)
    xc = xf - mu
    y = xc * lax.rsqrt(jnp.mean(xc * xc, axis=-1, keepdims=True) + EPS)
    return (y * g.astype(jnp.float32) + b.astype(jnp.float32)).astype(x.dtype)


def swiglu(x, w_gate, w_up, w_down):
    return (jax.nn.silu(x @ w_gate) * (x @ w_up)) @ w_down


def hgrn2_chunked(q, k, v, logf, s0):
    B, T = q.shape[0], q.shape[1]
    C = min(CHUNK, T)
    n = -(-T // C)
    pad = n * C - T

    def prep(a):
        a = jnp.pad(a.astype(jnp.float32), ((0, 0), (0, pad), (0, 0), (0, 0)))
        return a.reshape(B, n, C, a.shape[2], a.shape[3]).transpose(1, 0, 3, 2, 4)

    qc, kc, vc, gc = prep(q), prep(k), prep(v), prep(logf)
    mask = jnp.tril(jnp.ones((C, C), dtype=bool))[:, :, None]

    def step(S, inp):
        qb, kb, vb, lfb = inp
        G = jnp.cumsum(lfb, axis=2)
        o_inter = jnp.einsum('bhtk,bhkv->bhtv', qb * jnp.exp(G), S)
        diff = G[:, :, :, None, :] - G[:, :, None, :, :]
        decay = jnp.exp(jnp.where(mask, diff, -jnp.inf))
        A = jnp.einsum('bhtk,bhsk,bhtsk->bhts', qb, kb, decay)
        o = o_inter + jnp.einsum('bhts,bhsv->bhtv', A, vb)
        G_last = G[:, :, -1]
        S_new = jnp.exp(G_last)[..., None] * S + jnp.einsum(
            'bhsk,bhsv->bhkv', kb * jnp.exp(G_last[:, :, None] - G), vb)
        return S_new, o

    S, o = lax.scan(step, s0.astype(jnp.float32), (qc, kc, vc, gc))
    o = o.transpose(1, 0, 3, 2, 4).reshape(B, n * C, q.shape[2], v.shape[3])[:, :T]
    return o, S


def mixer(xn, s_hgrn, conv_buf, lb, w_in, hgrn_g, conv_w, conv_b, ln_g, ln_b, w_out):
    B, T, _ = xn.shape
    proj = xn @ w_in
    q, f, i, g, u = jnp.split(proj, [D_A, 2 * D_A, 3 * D_A, 4 * D_A], axis=-1)

    lb = lb.astype(jnp.float32)
    logf = jnp.logaddexp(jnp.log(lb), jnp.log1p(-lb) + jax.nn.log_sigmoid(f.astype(jnp.float32)))
    k = -jnp.expm1(logf)
    q = jax.nn.silu(q)
    heads = lambda a, d: a.reshape(B, T, H_A, d)
    o, S = hgrn2_chunked(heads(q, DK_A), heads(k, DK_A), heads(i, DV_A), heads(logf, DK_A), s_hgrn)
    o = rmsnorm(o, hgrn_g.reshape(H_A, DV_A)).astype(xn.dtype) * jax.nn.silu(heads(g, DV_A))
    o_a = o.reshape(B, T, D_A)

    glu = u[..., :D_B] * jax.nn.sigmoid(u[..., D_B:])
    full = jnp.concatenate([conv_buf.astype(glu.dtype), glu], axis=1)
    yc = lax.conv_general_dilated(
        full, conv_w.astype(glu.dtype)[:, None, :], window_strides=(1,), padding='VALID',
        dimension_numbers=('NWC', 'WIO', 'NWC'), feature_group_count=D_B)
    new_buf = full[:, T:]
    o_b = jax.nn.silu(layernorm(yc + conv_b, ln_g, ln_b))

    out = jnp.concatenate([o_a, o_b], axis=-1) @ w_out
    return out, S, new_buf


def trunk(x, p, s_hgrn, s_conv,
          norm_ffn1, ffn1_w_gate, ffn1_w_up, ffn1_w_down,
          norm_mix, w_in, hgrn_lb, hgrn_norm, conv_w, conv_b, conv_ln_g, conv_ln_b, w_out,
          norm_ffn2, ffn2_w_gate, ffn2_w_up, ffn2_w_down,
          norm_ple, w_ple_gate, w_ple_in, norm_final):
    lbs = jnp.cumsum(jax.nn.softmax(hgrn_lb.astype(jnp.float32), axis=0), axis=0)
    lbs = lbs - lbs[0]
    h = x
    new_s, new_c = [], []
    for l in range(DEPTH):
        h = h + 0.5 * swiglu(rmsnorm(h, norm_ffn1[l]), ffn1_w_gate[l], ffn1_w_up[l], ffn1_w_down[l])
        m, S, buf = mixer(rmsnorm(h, norm_mix[l]), s_hgrn[l], s_conv[l], lbs[l], w_in[l],
                          hgrn_norm[l], conv_w[l], conv_b[l], conv_ln_g[l], conv_ln_b[l], w_out[l])
        h = h + m
        h = h + 0.5 * swiglu(rmsnorm(h, norm_ffn2[l]), ffn2_w_gate[l], ffn2_w_up[l], ffn2_w_down[l])
        h = h + jax.nn.sigmoid(rmsnorm(h, norm_ple[l]) @ w_ple_gate[l]) * (p[l] @ w_ple_in[l])
        new_s.append(S)
        new_c.append(buf)
    return rmsnorm(h, norm_final), jnp.stack(new_s), jnp.stack(new_c)


def setup_inputs(seed: int = 0) -> dict:
    key = jax.random.key(seed)
    ks = jax.random.split(key, 32)
    nrm = lambda k, shape, s: jax.random.normal(k, shape, jnp.float32) * s
    gain = lambda k, shape: 1.0 + 0.01 * jax.random.normal(k, shape, jnp.float32)
    return {
        "x_prompt": nrm(ks[0], (BATCH, SEQ, D_MODEL), 1.0),
        "x_sample": nrm(ks[1], (DEC_BATCH, DEC_SEQ, D_MODEL), 1.0),
        "state_hgrn": nrm(ks[2], (DEPTH, DEC_BATCH, H_A, DK_A, DV_A), 0.5),
        "state_conv": nrm(ks[3], (DEPTH, DEC_BATCH, CONV_W - 1, D_B), 0.5),
        "p_prompt": nrm(ks[4], (DEPTH, BATCH, SEQ, PLE_DIM), 1.0),
        "p_sample": nrm(ks[5], (DEPTH, DEC_BATCH, DEC_SEQ, PLE_DIM), 1.0),
        "norm_ffn1": gain(ks[6], (DEPTH, D_MODEL)),
        "ffn1_w_gate": nrm(ks[7], (DEPTH, D_MODEL, D_FF), D_MODEL ** -0.5),
        "ffn1_w_up": nrm(ks[8], (DEPTH, D_MODEL, D_FF), D_MODEL ** -0.5),
        "ffn1_w_down": nrm(ks[9], (DEPTH, D_FF, D_MODEL), D_FF ** -0.5),
        "norm_mix": gain(ks[10], (DEPTH, D_MODEL)),
        "w_in": nrm(ks[11], (DEPTH, D_MODEL, D_IN), D_MODEL ** -0.5),
        "hgrn_lb": nrm(ks[12], (DEPTH, D_A), 0.5),
        "hgrn_norm": gain(ks[13], (DEPTH, D_A)),
        "conv_w": nrm(ks[14], (DEPTH, CONV_W, D_B), CONV_W ** -0.5),
        "conv_b": nrm(ks[15], (DEPTH, D_B), 0.01),
        "conv_ln_g": gain(ks[16], (DEPTH, D_B)),
        "conv_ln_b": nrm(ks[17], (DEPTH, D_B), 0.01),
        "w_out": nrm(ks[18], (DEPTH, D_MIX, D_MODEL), D_MIX ** -0.5),
        "norm_ffn2": gain(ks[19], (DEPTH, D_MODEL)),
        "ffn2_w_gate": nrm(ks[20], (DEPTH, D_MODEL, D_FF), D_MODEL ** -0.5),
        "ffn2_w_up": nrm(ks[21], (DEPTH, D_MODEL, D_FF), D_MODEL ** -0.5),
        "ffn2_w_down": nrm(ks[22], (DEPTH, D_FF, D_MODEL), D_FF ** -0.5),
        "norm_ple": gain(ks[23], (DEPTH, D_MODEL)),
        "w_ple_gate": nrm(ks[24], (DEPTH, D_MODEL, D_MODEL), D_MODEL ** -0.5),
        "w_ple_in": nrm(ks[25], (DEPTH, PLE_DIM, D_MODEL), PLE_DIM ** -0.5),
        "norm_final": gain(ks[26], (D_MODEL,)),
    }


def reference(x_prompt, x_sample, state_hgrn, state_conv, p_prompt, p_sample,
              norm_ffn1, ffn1_w_gate, ffn1_w_up, ffn1_w_down,
              norm_mix, w_in, hgrn_lb, hgrn_norm, conv_w, conv_b, conv_ln_g, conv_ln_b, w_out,
              norm_ffn2, ffn2_w_gate, ffn2_w_up, ffn2_w_down,
              norm_ple, w_ple_gate, w_ple_in, norm_final):
    weights = (norm_ffn1, ffn1_w_gate, ffn1_w_up, ffn1_w_down,
               norm_mix, w_in, hgrn_lb, hgrn_norm, conv_w, conv_b, conv_ln_g, conv_ln_b, w_out,
               norm_ffn2, ffn2_w_gate, ffn2_w_up, ffn2_w_down,
               norm_ple, w_ple_gate, w_ple_in, norm_final)
    s0 = jnp.zeros((DEPTH, BATCH, H_A, DK_A, DV_A), jnp.float32)
    c0 = jnp.zeros((DEPTH, BATCH, CONV_W - 1, D_B), x_prompt.dtype)
    y_prompt, s_prompt, c_prompt = trunk(x_prompt, p_prompt, s0, c0, *weights)
    y_sample, s_sample, c_sample = trunk(x_sample, p_sample, state_hgrn, state_conv, *weights)
    return (y_prompt, y_sample, s_prompt, c_prompt, s_sample, c_sample)
```

```python
import functools

import jax
import jax.numpy as jnp
from jax import lax
from jax.experimental import pallas as pl
from jax.experimental.pallas import tpu as pltpu

F32 = jnp.float32
BF16 = jnp.bfloat16

EPS = 1e-6
N_HEADS = 4
D_HEAD = 128
D_GROUP = 512
CONV_W = 31
CHUNK = 64
SUB = 8
HIST = 32

ROW_TILE = 512
MIX_TILE = 512
FF_SPLIT = 2
SAMPLE_BLOCK = 8
VMEM_LIMIT = 56 * 1024 * 1024


def _dot(a, b):
    return jnp.dot(a, b, preferred_element_type=F32)


def _dot_nt(a, b):
    return lax.dot_general(a, b, (((1,), (1,)), ((), ())), preferred_element_type=F32)


def _dot_tn(a, b):
    return lax.dot_general(a, b, (((0,), (0,)), ((), ())), preferred_element_type=F32)


def _rms(x, g):
    return x * lax.rsqrt(jnp.mean(x * x, axis=-1, keepdims=True) + EPS) * g


def _silu(x):
    return x * jax.nn.sigmoid(x)


def _log_sigmoid(x):
    return jnp.minimum(x, 0.0) - jnp.log1p(jnp.exp(-jnp.abs(x)))


def _logaddexp(a, b):
    return jnp.maximum(a, b) + jnp.log1p(jnp.exp(-jnp.abs(a - b)))


def _lower_bound(lb_ref, layer):
    z = lb_ref[...]
    e = jnp.exp(z - jnp.max(z, axis=0, keepdims=True))
    sm = e / jnp.sum(e, axis=0, keepdims=True)
    c0 = sm[0:1, :]
    c = c0
    for i in range(1, layer + 1):
        c = c + sm[i:i + 1, :]
    return c - c0


def _in_projection(x, gm_ref, win_ref, lb_ref, layer):
    xn = _rms(x, gm_ref[...]).astype(BF16)
    proj = lambda j: _dot(xn, win_ref[:, j * D_GROUP:(j + 1) * D_GROUP])
    q = _silu(proj(0))
    lb = _lower_bound(lb_ref, layer)
    logf = _logaddexp(jnp.log(lb), jnp.log1p(-lb) + _log_sigmoid(proj(1)))
    v = proj(2)
    gs = _silu(proj(3))
    glu = proj(4) * jax.nn.sigmoid(proj(5))
    return q, logf, v, gs, glu


def _out_projection(x, oa, yc, cb_ref, lng_ref, lnb_ref, wout_ref):
    y = yc + cb_ref[...]
    mu = jnp.mean(y, axis=-1, keepdims=True)
    yc0 = y - mu
    ln = yc0 * lax.rsqrt(jnp.mean(yc0 * yc0, axis=-1, keepdims=True) + EPS) * lng_ref[...] + lnb_ref[...]
    ob = _silu(ln)
    out = _dot(oa.astype(BF16), wout_ref[0:D_GROUP, :]) + _dot(ob.astype(BF16), wout_ref[D_GROUP:2 * D_GROUP, :])
    return x + out


def _row_kernel(*refs, n_prompt_tiles, d_ff, do_ple, final_norm):
    it = iter(refs)
    xp_ref, xs_ref = next(it), next(it)
    pp_ref, ps_ref = (next(it), next(it)) if do_ple else (None, None)
    g_ref, wg_ref, wu_ref, wd_ref = next(it), next(it), next(it), next(it)
    gp_ref, wpg_ref, wpi_ref = (next(it), next(it), next(it)) if do_ple else (None, None, None)
    gf_ref = next(it) if final_norm else None
    op_ref, os_ref = next(it), next(it)

    ff_chunk = d_ff // FF_SPLIT

    def process(x_ref, p_ref, o_ref):
        x = x_ref[...]
        xn = _rms(x, g_ref[...]).astype(BF16)
        acc = None
        for c in range(0, d_ff, ff_chunk):
            gate = _dot(xn, wg_ref[:, c:c + ff_chunk])
            up = _dot(xn, wu_ref[:, c:c + ff_chunk])
            part = _dot((_silu(gate) * up).astype(BF16), wd_ref[c:c + ff_chunk, :])
            acc = part if acc is None else acc + part
        h = x + 0.5 * acc
        if do_ple:
            gate = jax.nn.sigmoid(_dot(_rms(h, gp_ref[...]).astype(BF16), wpg_ref[...]))
            h = h + gate * _dot(p_ref[...].astype(BF16), wpi_ref[...])
        if final_norm:
            h = _rms(h, gf_ref[...])
        o_ref[...] = h

    i = pl.program_id(0)

    @pl.when(i < n_prompt_tiles)
    def _():
        process(xp_ref, pp_ref, op_ref)

    @pl.when(i == n_prompt_tiles)
    def _():
        process(xs_ref, ps_ref, os_ref)


def _resident(shape, index_map):
    return pl.BlockSpec(shape, index_map, pipeline_mode=pl.Buffered(1))


def _row_call(hp, hs, layer, g, wg, wu, wd, ple=None, final_gain=None):
    n_p, d = hp.shape
    n_s = hs.shape[0]
    d_ff = wg.shape[-1]
    assert n_p % ROW_TILE == 0 and d_ff % (FF_SPLIT * 128) == 0
    npt = n_p // ROW_TILE
    do_ple = ple is not None
    final_norm = final_gain is not None

    ptile = lambda i: (jnp.minimum(i, npt - 1), 0)
    whole = lambda i: (0, 0)
    lsel3 = lambda i: (layer, 0, 0)

    in_specs = [pl.BlockSpec((ROW_TILE, d), ptile), pl.BlockSpec((n_s, d), whole)]
    args = [hp, hs]
    if do_ple:
        pp, ps, gp, wpg, wpi = ple
        d_p = pp.shape[-1]
        in_specs += [pl.BlockSpec((None, ROW_TILE, d_p), lambda i: (layer, jnp.minimum(i, npt - 1), 0)),
                     pl.BlockSpec((None, n_s, d_p), lsel3)]
        args += [pp, ps]
    in_specs += [pl.BlockSpec((None, 1, d), lsel3),
                 _resident((None, d, d_ff), lsel3), _resident((None, d, d_ff), lsel3),
                 _resident((None, d_ff, d), lsel3)]
    args += [g, wg, wu, wd]
    if do_ple:
        in_specs += [pl.BlockSpec((None, 1, d), lsel3), _resident((None, d, d), lsel3),
                     _resident((None, d_p, d), lsel3)]
        args += [gp, wpg, wpi]
    if final_norm:
        in_specs += [pl.BlockSpec((1, d), whole)]
        args += [final_gain]

    return pl.pallas_call(
        functools.partial(_row_kernel, n_prompt_tiles=npt, d_ff=d_ff, do_ple=do_ple, final_norm=final_norm),
        grid=(npt + 1,),
        in_specs=in_specs,
        out_specs=[pl.BlockSpec((ROW_TILE, d), ptile), pl.BlockSpec((n_s, d), whole)],
        out_shape=[jax.ShapeDtypeStruct(hp.shape, F32), jax.ShapeDtypeStruct(hs.shape, F32)],
        compiler_params=pltpu.CompilerParams(dimension_semantics=("arbitrary",), vmem_limit_bytes=VMEM_LIMIT),
        name=f"rows_l{layer}_{'ple' if do_ple else 'ffn'}",
    )(*args)


def _chunk_masks():
    t = lax.broadcasted_iota(jnp.int32, (CHUNK, CHUNK), 0)
    s = lax.broadcasted_iota(jnp.int32, (CHUNK, CHUNK), 1)
    levels = {}
    bs = CHUNK // 2
    while bs >= SUB:
        sh = bs.bit_length() - 1
        levels[bs] = ((t >> (sh + 1)) == (s >> (sh + 1))) & (((t >> sh) & 1) == 1) & (((s >> sh) & 1) == 0)
        bs //= 2
    diag = [(s == ((t >> 3) << 3) + j) & ((t & (SUB - 1)) >= j) for j in range(SUB)]
    return levels, diag


def _hgrn_chunk(q, k, lf, v, st, masks):
    levels, diag = masks
    nb = CHUNK // SUB
    row = lax.broadcasted_iota(jnp.int32, (SUB, D_HEAD), 0)
    g_blocks, ends = [], []
    carry = None
    for b in range(nb):
        xb = lf[SUB * b:SUB * (b + 1), :]
        for sft in (1, 2, 4):
            xb = xb + jnp.where(row >= sft, pltpu.roll(xb, sft, axis=0), 0.0)
        if carry is not None:
            xb = xb + carry
        carry = jnp.broadcast_to(xb[SUB - 1:SUB, :], (SUB, D_HEAD))
        g_blocks.append(xb)
        ends.append(carry)
    g = jnp.concatenate(g_blocks, axis=0)

    o = _dot_nt((q * jnp.exp(g)).astype(BF16), st.astype(BF16))

    a = jnp.zeros((CHUNK, CHUNK), F32)
    for bs, mask in levels.items():
        n = bs // SUB
        ref = jnp.concatenate([ends[(b // (2 * n)) * (2 * n) + n - 1] for b in range(nb)], axis=0)
        e = jnp.exp(-jnp.abs(g - ref))
        a = jnp.where(mask, _dot_nt((q * e).astype(BF16), (k * e).astype(BF16)), a)
    for j in range(SUB):
        cols = []
        for b in range(nb):
            gb = g_blocks[b]
            d = jnp.exp(jnp.minimum(gb - gb[j:j + 1, :], 0.0))
            cols.append(q[SUB * b:SUB * (b + 1), :] * k[SUB * b + j:SUB * b + j + 1, :] * d)
        col = jnp.sum(jnp.concatenate(cols, axis=0), axis=-1, keepdims=True)
        a = jnp.where(diag[j], col, a)
    o = o + _dot(a.astype(BF16), v.astype(BF16))

    g_last = jnp.concatenate([ends[nb - 1]] * nb, axis=0)
    kdec = k * jnp.exp(g_last - g)
    st_new = st * jnp.exp(ends[nb - 1][0:1, :]) + _dot_tn(v.astype(BF16), kdec.astype(BF16))
    return o, st_new


def _mix_prompt_kernel(h_ref, gm_ref, win_ref, lb_ref, hn_ref, cw_ref, cb_ref, lng_ref, lnb_ref, wout_ref,
                       o_ref, s_out_ref, c_out_ref,
                       q_s, k_s, lf_s, v_s, g_s, oa_s, gbuf, st_s, *, layer, tt):
    t = pl.program_id(1)

    @pl.when(t == 0)
    def _():
        st_s[...] = jnp.zeros_like(st_s)
        gbuf[0:HIST, :] = jnp.zeros((HIST, D_GROUP), F32)

    x = h_ref[...]
    q, logf, v, gs, glu = _in_projection(x, gm_ref, win_ref, lb_ref, layer)
    q_s[...] = q
    lf_s[...] = logf
    k_s[...] = 1.0 - jnp.exp(logf)
    v_s[...] = v
    g_s[...] = gs
    gbuf[HIST:HIST + tt, :] = glu

    def chunk_body(c, carry):
        r0 = pl.multiple_of(c * CHUNK, CHUNK)
        rows = pl.ds(r0, CHUNK)
        masks = _chunk_masks()
        for h in range(N_HEADS):
            hs = slice(h * D_HEAD, (h + 1) * D_HEAD)
            o, st_new = _hgrn_chunk(q_s[rows, hs], k_s[rows, hs], lf_s[rows, hs], v_s[rows, hs], st_s[h], masks)
            st_s[h] = st_new
            oa_s[rows, hs] = _rms(o, hn_ref[:, hs]) * g_s[rows, hs]
        return carry

    lax.fori_loop(0, tt // CHUNK, chunk_body, 0)

    yc = jnp.zeros((tt, D_GROUP), F32)
    for w in range(CONV_W):
        yc = yc + gbuf[pl.ds(HIST - (CONV_W - 1) + w, tt), :] * cw_ref[w:w + 1, :]
    o_ref[...] = _out_projection(x, oa_s[...], yc, cb_ref, lng_ref, lnb_ref, wout_ref)

    @pl.when(t == pl.num_programs(1) - 1)
    def _():
        c_out_ref[...] = gbuf[pl.ds(tt + HIST - (CONV_W - 1), CONV_W - 1), :]
        for h in range(N_HEADS):
            s_out_ref[h] = st_s[h].T

    gbuf[0:HIST, :] = gbuf[tt:tt + HIST, :]


def _mixer_weight_specs(layer, d, d_in, grid_rank):
    zeros = (0,) * (grid_rank - 0)
    lsel3 = lambda *_: (layer, 0, 0)
    whole2 = lambda *_: (0, 0)
    return [pl.BlockSpec((None, 1, d), lsel3),
            _resident((None, d, d_in), lsel3),
            pl.BlockSpec((N_HEADS, D_GROUP), whole2),
            pl.BlockSpec((None, 1, D_GROUP), lsel3),
            pl.BlockSpec((None, CONV_W, D_GROUP), lsel3),
            pl.BlockSpec((None, 1, D_GROUP), lsel3),
            pl.BlockSpec((None, 1, D_GROUP), lsel3),
            pl.BlockSpec((None, 1, D_GROUP), lsel3),
            _resident((None, 2 * D_GROUP, d), lsel3)]


def _mix_prompt_call(hp, layer, batch, seq, mixw):
    n_p, d = hp.shape
    d_in = mixw[1].shape[-1]
    tt = MIX_TILE
    assert seq % tt == 0 and tt % CHUNK == 0 and tt >= HIST
    nt = seq // tt
    tile = lambda b, t: (b * nt + t, 0)
    scratch = [pltpu.VMEM((tt, D_GROUP), F32) for _ in range(6)]
    scratch += [pltpu.VMEM((tt + HIST, D_GROUP), F32), pltpu.VMEM((N_HEADS, D_HEAD, D_HEAD), F32)]
    return pl.pallas_call(
        functools.partial(_mix_prompt_kernel, layer=layer, tt=tt),
        grid=(batch, nt),
        in_specs=[pl.BlockSpec((tt, d), tile)] + _mixer_weight_specs(layer, d, d_in, 2),
        out_specs=[pl.BlockSpec((tt, d), tile),
                   pl.BlockSpec((None, N_HEADS, D_HEAD, D_HEAD), lambda b, t: (b, 0, 0, 0)),
                   pl.BlockSpec((None, CONV_W - 1, D_GROUP), lambda b, t: (b, 0, 0))],
        out_shape=[jax.ShapeDtypeStruct(hp.shape, F32),
                   jax.ShapeDtypeStruct((batch, N_HEADS, D_HEAD, D_HEAD), F32),
                   jax.ShapeDtypeStruct((batch, CONV_W - 1, D_GROUP), F32)],
        scratch_shapes=scratch,
        compiler_params=pltpu.CompilerParams(dimension_semantics=("arbitrary", "arbitrary"),
                                             vmem_limit_bytes=VMEM_LIMIT),
        name=f"mix_prompt_l{layer}",
    )(hp, *mixw)


def _mix_sample_kernel(h_ref, gm_ref, win_ref, lb_ref, hn_ref, cw_ref, cb_ref, lng_ref, lnb_ref, wout_ref,
                       s_ref, cs_ref, o_ref, s_out_ref, cs_out_ref,
                       q_s, f_s, v_s, g_s, glu_s, oraw_s, yc_s, blk_o, blk_y, *, layer):
    i = pl.program_id(0)
    sb = SAMPLE_BLOCK

    @pl.when(i == 0)
    def _():
        q, logf, v, gs, glu = _in_projection(h_ref[...], gm_ref, win_ref, lb_ref, layer)
        q_s[...] = q
        f_s[...] = jnp.exp(logf)
        v_s[...] = v
        g_s[...] = gs
        glu_s[...] = glu

    r0 = pl.multiple_of(i * sb, sb)
    rows = pl.ds(r0, sb)
    f8, q8, v8, glu8 = f_s[rows, :], q_s[rows, :], v_s[rows, :], glu_s[rows, :]
    pad = jnp.zeros((D_HEAD - 3 * sb, D_HEAD), F32)
    for h in range(N_HEADS):
        hs = slice(h * D_HEAD, (h + 1) * D_HEAD)
        fh = f8[:, hs]
        cols = jnp.concatenate([fh, 1.0 - fh, q8[:, hs], pad], axis=0).T
        for j in range(sb):
            sn = cols[:, j:j + 1] * s_ref[j, h] + cols[:, sb + j:sb + j + 1] * v8[j:j + 1, hs]
            s_out_ref[j, h] = sn
            blk_o[j:j + 1, hs] = jnp.sum(sn * cols[:, 2 * sb + j:2 * sb + j + 1], axis=0, keepdims=True)
    for j in range(sb):
        glu_j = glu8[j:j + 1, :]
        blk_y[j:j + 1, :] = (jnp.sum(cs_ref[j] * cw_ref[0:CONV_W - 1, :], axis=0, keepdims=True)
                             + glu_j * cw_ref[CONV_W - 1:CONV_W, :])
        cs_out_ref[j, 0:CONV_W - 2, :] = cs_ref[j, 1:CONV_W - 1, :]
        cs_out_ref[j, CONV_W - 2:CONV_W - 1, :] = glu_j
    oraw_s[rows, :] = blk_o[...]
    yc_s[rows, :] = blk_y[...]

    @pl.when(i == pl.num_programs(0) - 1)
    def _():
        oraw = oraw_s[...]
        oa = jnp.concatenate(
            [_rms(oraw[:, h * D_HEAD:(h + 1) * D_HEAD], hn_ref[:, h * D_HEAD:(h + 1) * D_HEAD])
             for h in range(N_HEADS)], axis=-1) * g_s[...]
        o_ref[...] = _out_projection(h_ref[...], oa, yc_s[...], cb_ref, lng_ref, lnb_ref, wout_ref)


def _mix_sample_call(hs, layer, state_hgrn, state_conv, mixw):
    n_s, d = hs.shape
    d_in = mixw[1].shape[-1]
    sb = SAMPLE_BLOCK
    assert n_s % sb == 0 and 3 * sb <= D_HEAD
    whole = lambda i: (0, 0)
    scratch = [pltpu.VMEM((n_s, D_GROUP), F32) for _ in range(7)]
    scratch += [pltpu.VMEM((sb, D_GROUP), F32), pltpu.VMEM((sb, D_GROUP), F32)]
    return pl.pallas_call(
        functools.partial(_mix_sample_kernel, layer=layer),
        grid=(n_s // sb,),
        in_specs=[pl.BlockSpec((n_s, d), whole)] + _mixer_weight_specs(layer, d, d_in, 1)
        + [pl.BlockSpec((None, sb, N_HEADS, D_HEAD, D_HEAD), lambda i: (layer, i, 0, 0, 0)),
           pl.BlockSpec((None, sb, CONV_W - 1, D_GROUP), lambda i: (layer, i, 0, 0))],
        out_specs=[pl.BlockSpec((n_s, d), whole),
                   pl.BlockSpec((sb, N_HEADS, D_HEAD, D_HEAD), lambda i: (i, 0, 0, 0)),
                   pl.BlockSpec((sb, CONV_W - 1, D_GROUP), lambda i: (i, 0, 0))],
        out_shape=[jax.ShapeDtypeStruct(hs.shape, F32),
                   jax.ShapeDtypeStruct(state_hgrn.shape[1:], F32),
                   jax.ShapeDtypeStruct(state_conv.shape[1:], F32)],
        scratch_shapes=scratch,
        compiler_params=pltpu.CompilerParams(dimension_semantics=("arbitrary",), vmem_limit_bytes=VMEM_LIMIT),
        name=f"mix_sample_l{layer}",
    )(hs, *mixw, state_hgrn, state_conv)


def kernel(x_prompt, x_sample, state_hgrn, state_conv, p_prompt, p_sample, norm_ffn1, ffn1_w_gate, ffn1_w_up, ffn1_w_down, norm_mix, w_in, hgrn_lb, hgrn_norm, conv_w, conv_b, conv_ln_g, conv_ln_b, w_out, norm_ffn2, ffn2_w_gate, ffn2_w_up, ffn2_w_down, norm_ple, w_ple_gate, w_ple_in, norm_final):
    batch, seq, d = x_prompt.shape
    n_s = x_sample.shape[0] * x_sample.shape[1]
    depth = w_in.shape[0]
    assert w_in.shape[-1] == 6 * D_GROUP and hgrn_lb.shape == (depth, D_GROUP) and depth == N_HEADS

    bf = lambda w: w.astype(BF16)
    row3 = lambda a: a.reshape(depth, 1, a.shape[-1])
    ffn1 = (row3(norm_ffn1), bf(ffn1_w_gate), bf(ffn1_w_up), bf(ffn1_w_down))
    ffn2 = (row3(norm_ffn2), bf(ffn2_w_gate), bf(ffn2_w_up), bf(ffn2_w_down))
    mixw = (row3(norm_mix), bf(w_in), hgrn_lb, row3(hgrn_norm), conv_w, row3(conv_b), row3(conv_ln_g),
            row3(conv_ln_b), bf(w_out))
    pp = p_prompt.reshape(depth, batch * seq, p_prompt.shape[-1])
    ps = p_sample.reshape(depth, n_s, p_sample.shape[-1])
    ple = (pp, ps, row3(norm_ple), bf(w_ple_gate), bf(w_ple_in))
    gain_final = norm_final.reshape(1, d)

    hp = x_prompt.reshape(batch * seq, d)
    hs = x_sample.reshape(n_s, d)
    s_p, c_p, s_s, c_s = [], [], [], []
    for layer in range(depth):
        hp, hs = _row_call(hp, hs, layer, *ffn1)
        hp, sp, cp = _mix_prompt_call(hp, layer, batch, seq, mixw)
        hs, ss, cs = _mix_sample_call(hs, layer, state_hgrn, state_conv, mixw)
        hp, hs = _row_call(hp, hs, layer, *ffn2, ple=ple, final_gain=gain_final if layer == depth - 1 else None)
        s_p.append(sp)
        c_p.append(cp)
        s_s.append(ss)
        c_s.append(cs)
    return (hp.reshape(x_prompt.shape), hs.reshape(x_sample.shape),
            jnp.stack(s_p), jnp.stack(c_p), jnp.stack(s_s), jnp.stack(c_s))
```

```python
import functools

import jax
import jax.numpy as jnp
from jax import lax
from jax.experimental import pallas as pl
from jax.experimental.pallas import tpu as pltpu

F32 = jnp.float32
BF16 = jnp.bfloat16

EPS = 1e-6
LN2 = 0.6931471805599453
N_HEADS = 4
D_HEAD = 128
D_GROUP = 512
N_SLAB = D_GROUP // D_HEAD
CONV_W = 31
CHUNK = 64
SUB = 8
HIST = 32
PHASES = 4
CONV_BLOCK = PHASES * SUB

ROW_TILE = 512
MIX_TILE = 512
FF_SPLIT = 2
SAMPLE_BLOCK = 8
VMEM_LIMIT = 56 * 1024 * 1024


def _dot(a, b):
    return jnp.dot(a, b, preferred_element_type=F32)


def _dot_nt(a, b):
    return lax.dot_general(a, b, (((1,), (1,)), ((), ())), preferred_element_type=F32)


def _dot_tn(a, b):
    return lax.dot_general(a, b, (((0,), (0,)), ((), ())), preferred_element_type=F32)


def _rms(x, g):
    return x * lax.rsqrt(jnp.mean(x * x, axis=-1, keepdims=True) + EPS) * g


def _silu(x):
    return x * jax.nn.sigmoid(x)


def _lower_bound(lb_ref, layer):
    z = lb_ref[...]
    e = jnp.exp(z - jnp.max(z, axis=0, keepdims=True))
    sm = e / jnp.sum(e, axis=0, keepdims=True)
    c0 = sm[0:1, :]
    c = c0
    for i in range(1, layer + 1):
        c = c + sm[i:i + 1, :]
    return c - c0


def _gates(fr, lb):
    u = jnp.exp(-jnp.abs(fr))
    den = 1.0 / (1.0 + u)
    pos = fr >= 0.0
    f = jnp.where(pos, 1.0 + lb * u, u + lb) * den
    k = (1.0 - lb) * jnp.where(pos, u, 1.0) * den
    return f, k


def _layernorm_silu(y, g, b):
    mu = jnp.mean(y, axis=-1, keepdims=True)
    yc = y - mu
    return _silu(yc * lax.rsqrt(jnp.mean(yc * yc, axis=-1, keepdims=True) + EPS) * g + b)


def _row_kernel(*refs, n_prompt_tiles, d_ff, do_ple, final_norm):
    it = iter(refs)
    xp_ref, xs_ref = next(it), next(it)
    pp_ref, ps_ref = (next(it), next(it)) if do_ple else (None, None)
    g_ref, wg_ref, wu_ref, wd_ref = next(it), next(it), next(it), next(it)
    gp_ref, wpg_ref, wpi_ref = (next(it), next(it), next(it)) if do_ple else (None, None, None)
    gf_ref = next(it) if final_norm else None
    op_ref, os_ref = next(it), next(it)

    ff_chunk = d_ff // FF_SPLIT

    def process(x_ref, p_ref, o_ref):
        x = x_ref[...]
        xn = _rms(x, g_ref[...]).astype(BF16)
        acc = None
        for c in range(0, d_ff, ff_chunk):
            gate = _dot(xn, wg_ref[:, c:c + ff_chunk])
            up = _dot(xn, wu_ref[:, c:c + ff_chunk])
            part = _dot((_silu(gate) * up).astype(BF16), wd_ref[c:c + ff_chunk, :])
            acc = part if acc is None else acc + part
        h = x + 0.5 * acc
        if do_ple:
            gate = jax.nn.sigmoid(_dot(_rms(h, gp_ref[...]).astype(BF16), wpg_ref[...]))
            h = h + gate * _dot(p_ref[...].astype(BF16), wpi_ref[...])
        if final_norm:
            h = _rms(h, gf_ref[...])
        o_ref[...] = h

    i = pl.program_id(0)

    @pl.when(i < n_prompt_tiles)
    def _():
        process(xp_ref, pp_ref, op_ref)

    @pl.when(i == n_prompt_tiles)
    def _():
        process(xs_ref, ps_ref, os_ref)


def _resident(shape, index_map):
    return pl.BlockSpec(shape, index_map, pipeline_mode=pl.Buffered(1))


def _row_call(hp, hs, layer, g, wg, wu, wd, ple=None, final_gain=None):
    n_p, d = hp.shape
    n_s = hs.shape[0]
    d_ff = wg.shape[-1]
    assert n_p % ROW_TILE == 0 and d_ff % (FF_SPLIT * 128) == 0
    npt = n_p // ROW_TILE
    do_ple = ple is not None
    final_norm = final_gain is not None

    ptile = lambda i: (jnp.minimum(i, npt - 1), 0)
    whole = lambda i: (0, 0)
    lsel3 = lambda i: (layer, 0, 0)

    in_specs = [pl.BlockSpec((ROW_TILE, d), ptile), pl.BlockSpec((n_s, d), whole)]
    args = [hp, hs]
    if do_ple:
        pp, ps, gp, wpg, wpi = ple
        d_p = pp.shape[-1]
        in_specs += [pl.BlockSpec((None, ROW_TILE, d_p), lambda i: (layer, jnp.minimum(i, npt - 1), 0)),
                     pl.BlockSpec((None, n_s, d_p), lsel3)]
        args += [pp, ps]
    in_specs += [pl.BlockSpec((None, 1, d), lsel3),
                 _resident((None, d, d_ff), lsel3), _resident((None, d, d_ff), lsel3),
                 _resident((None, d_ff, d), lsel3)]
    args += [g, wg, wu, wd]
    if do_ple:
        in_specs += [pl.BlockSpec((None, 1, d), lsel3), _resident((None, d, d), lsel3),
                     _resident((None, d_p, d), lsel3)]
        args += [gp, wpg, wpi]
    if final_norm:
        in_specs += [pl.BlockSpec((1, d), whole)]
        args += [final_gain]

    return pl.pallas_call(
        functools.partial(_row_kernel, n_prompt_tiles=npt, d_ff=d_ff, do_ple=do_ple, final_norm=final_norm),
        grid=(npt + 1,),
        in_specs=in_specs,
        out_specs=[pl.BlockSpec((ROW_TILE, d), ptile), pl.BlockSpec((n_s, d), whole)],
        out_shape=[jax.ShapeDtypeStruct(hp.shape, F32), jax.ShapeDtypeStruct(hs.shape, F32)],
        compiler_params=pltpu.CompilerParams(dimension_semantics=("arbitrary",), vmem_limit_bytes=VMEM_LIMIT),
        name=f"rows_l{layer}_{'ple' if do_ple else 'ffn'}",
    )(*args)


N_LEVELS = CHUNK.bit_length() - 1


def _score_owner():
    t = lax.broadcasted_iota(jnp.int32, (CHUNK, CHUNK), 0)
    s = lax.broadcasted_iota(jnp.int32, (CHUNK, CHUNK), 1)
    return jnp.where(s > t, -1, jnp.where(s == t, N_LEVELS, 31 - lax.clz(t ^ s)))


def _hgrn_scores(qr, fr, lb, owner):
    nb = CHUNK // SUB
    row = lax.broadcasted_iota(jnp.int32, (SUB, D_HEAD), 0)
    bcast = lambda xb, r: jnp.broadcast_to(xb[r:r + 1, :], (SUB, D_HEAD))

    q = _silu(qr)
    f, k = _gates(fr, lb)
    lf = jnp.maximum(jnp.log(f), jnp.minimum(fr, 0.0) - LN2)

    g_blocks, ends = [], []
    carry = None
    for b in range(nb):
        xb = lf[SUB * b:SUB * (b + 1), :]
        for sft in (1, 2, 4):
            xb = xb + jnp.where(row >= sft, pltpu.roll(xb, sft, axis=0), 0.0)
        if carry is not None:
            xb = xb + carry
        carry = bcast(xb, SUB - 1)
        g_blocks.append(xb)
        ends.append(carry)
    g = jnp.concatenate(g_blocks, axis=0)

    refs = {}
    refs[2] = [jnp.where(row < 4, bcast(gb, 1), bcast(gb, 5)) for gb in g_blocks]
    refs[4] = [bcast(gb, 3) for gb in g_blocks]
    bs = SUB
    while bs < CHUNK:
        n = bs // SUB
        refs[bs] = [ends[(b // (2 * n)) * (2 * n) + n - 1] for b in range(nb)]
        bs *= 2

    odd = (lax.broadcasted_iota(jnp.int32, (CHUNK, D_HEAD), 0) & 1) == 1
    a = jnp.where(owner == 0, _dot_nt((q * jnp.where(odd, f, 1.0)).astype(BF16), k.astype(BF16)), 0.0)
    for level in range(1, N_LEVELS):
        e = jnp.exp(-jnp.abs(g - jnp.concatenate(refs[1 << level], axis=0)))
        a = jnp.where(owner == level, _dot_nt((q * e).astype(BF16), (k * e).astype(BF16)), a)
    a = jnp.where(owner == N_LEVELS, jnp.sum(q * k, axis=-1, keepdims=True), a)

    g_last = jnp.concatenate([ends[nb - 1]] * nb, axis=0)
    q_dec = q * jnp.exp(g)
    k_dec = k * jnp.exp(g_last - g)
    return a.astype(BF16), q_dec.astype(BF16), k_dec.astype(BF16), jnp.exp(ends[nb - 1])


def _mix_prompt_kernel(h_ref, gm_ref, win_ref, lb_ref, hn_ref, cw_ref, cb_ref, lng_ref, lnb_ref, wout_ref,
                       o_ref, s_out_ref, c_out_ref,
                       proj_s, oa_s, yc_s, gbuf, st_s, own_s, a_s, qd_s, kd_s, dec_s, *, layer, tt):
    t = pl.program_id(1)

    @pl.when(t == 0)
    def _():
        st_s[...] = jnp.zeros_like(st_s)
        gbuf[:, 0:HIST, :] = jnp.zeros((N_SLAB, HIST, D_HEAD), F32)

    own_s[...] = _score_owner()
    x = h_ref[...]
    proj_s[...] = _dot(_rms(x, gm_ref[...]).astype(BF16), win_ref[...])
    lb = _lower_bound(lb_ref, layer)
    col = lambda j, h: slice(j * D_GROUP + h * D_HEAD, j * D_GROUP + (h + 1) * D_HEAD)
    chunk_rows = lambda c: pl.ds(pl.multiple_of(c * CHUNK, CHUNK), CHUNK)

    def stash_scores(c):
        rows = chunk_rows(c)
        owner = own_s[...]
        for h in range(N_HEADS):
            hs = slice(h * D_HEAD, (h + 1) * D_HEAD)
            a_s[h], qd_s[h], kd_s[h], dec_s[h] = _hgrn_scores(proj_s[rows, col(0, h)], proj_s[rows, col(1, h)],
                                                               lb[:, hs], owner)

    def advance_state(c):
        rows = chunk_rows(c)
        for h in range(N_HEADS):
            hs = slice(h * D_HEAD, (h + 1) * D_HEAD)
            v = proj_s[rows, col(2, h)].astype(BF16)
            st = st_s[h]
            o = _dot_nt(qd_s[h], st.astype(BF16)) + _dot(a_s[h], v)
            st_s[h] = st * dec_s[h][0:1, :] + _dot_tn(v, kd_s[h])
            oa_s[rows, hs] = _rms(o, hn_ref[:, hs]) * _silu(proj_s[rows, col(3, h)])

    def chunk_body(c, carry):
        advance_state(c - 1)
        stash_scores(c)
        return carry

    n_chunks = tt // CHUNK
    stash_scores(0)
    lax.fori_loop(1, n_chunks, chunk_body, 0)
    advance_state(n_chunks - 1)

    for j in range(N_SLAB):
        gbuf[j, HIST:HIST + tt, :] = proj_s[:, col(4, j)] * jax.nn.sigmoid(proj_s[:, col(5, j)])

    def conv_body(i, carry):
        base = pl.multiple_of(i * CONV_BLOCK, CONV_BLOCK)
        for j in range(N_SLAB):
            js = slice(j * D_HEAD, (j + 1) * D_HEAD)
            accs = [jnp.broadcast_to(cb_ref[:, js], (SUB, D_HEAD)) for _ in range(PHASES)]
            for w in range(CONV_W):
                cw = cw_ref[w:w + 1, js]
                for p in range(PHASES):
                    start = base + (HIST - (CONV_W - 1) + p + w)
                    accs[p] = accs[p] + gbuf[j, pl.ds(start, SUB, stride=PHASES), :] * cw
            for p in range(PHASES):
                yc_s[j, pl.ds(base + p, SUB, stride=PHASES), :] = accs[p]
        return carry

    lax.fori_loop(0, tt // CONV_BLOCK, conv_body, 0)

    ob = _layernorm_silu(jnp.concatenate([yc_s[j] for j in range(N_SLAB)], axis=-1), lng_ref[...], lnb_ref[...])
    o_ref[...] = (x + _dot(oa_s[...].astype(BF16), wout_ref[0:D_GROUP, :])
                  + _dot(ob.astype(BF16), wout_ref[D_GROUP:2 * D_GROUP, :]))

    @pl.when(t == pl.num_programs(1) - 1)
    def _():
        lo = tt + HIST - (CONV_W - 1)
        c_out_ref[...] = jnp.concatenate([gbuf[j, lo:lo + CONV_W - 1, :] for j in range(N_SLAB)], axis=-1)
        for h in range(N_HEADS):
            s_out_ref[h] = st_s[h].T

    gbuf[:, 0:HIST, :] = gbuf[:, tt:tt + HIST, :]


def _mixer_weight_specs(layer, depth, d, d_in):
    lsel3 = lambda *_: (layer, 0, 0)
    whole2 = lambda *_: (0, 0)
    return [pl.BlockSpec((None, 1, d), lsel3),
            _resident((None, d, d_in), lsel3),
            pl.BlockSpec((depth, D_GROUP), whole2),
            pl.BlockSpec((None, 1, D_GROUP), lsel3),
            pl.BlockSpec((None, CONV_W, D_GROUP), lsel3),
            pl.BlockSpec((None, 1, D_GROUP), lsel3),
            pl.BlockSpec((None, 1, D_GROUP), lsel3),
            pl.BlockSpec((None, 1, D_GROUP), lsel3),
            _resident((None, 2 * D_GROUP, d), lsel3)]


def _mix_prompt_call(hp, layer, batch, seq, mixw):
    n_p, d = hp.shape
    depth, _, d_in = mixw[1].shape
    tt = MIX_TILE
    assert seq % tt == 0 and tt % CHUNK == 0 and tt % CONV_BLOCK == 0 and tt >= HIST
    nt = seq // tt
    tile = lambda b, t: (b * nt + t, 0)
    scratch = [pltpu.VMEM((tt, d_in), F32),
               pltpu.VMEM((tt, D_GROUP), F32),
               pltpu.VMEM((N_SLAB, tt, D_HEAD), F32),
               pltpu.VMEM((N_SLAB, tt + HIST, D_HEAD), F32),
               pltpu.VMEM((N_HEADS, D_HEAD, D_HEAD), F32),
               pltpu.VMEM((CHUNK, CHUNK), jnp.int32),
               pltpu.VMEM((N_HEADS, CHUNK, CHUNK), BF16),
               pltpu.VMEM((N_HEADS, CHUNK, D_HEAD), BF16),
               pltpu.VMEM((N_HEADS, CHUNK, D_HEAD), BF16),
               pltpu.VMEM((N_HEADS, SUB, D_HEAD), F32)]
    return pl.pallas_call(
        functools.partial(_mix_prompt_kernel, layer=layer, tt=tt),
        grid=(batch, nt),
        in_specs=[pl.BlockSpec((tt, d), tile)] + _mixer_weight_specs(layer, depth, d, d_in),
        out_specs=[pl.BlockSpec((tt, d), tile),
                   pl.BlockSpec((None, N_HEADS, D_HEAD, D_HEAD), lambda b, t: (b, 0, 0, 0)),
                   pl.BlockSpec((None, CONV_W - 1, D_GROUP), lambda b, t: (b, 0, 0))],
        out_shape=[jax.ShapeDtypeStruct(hp.shape, F32),
                   jax.ShapeDtypeStruct((batch, N_HEADS, D_HEAD, D_HEAD), F32),
                   jax.ShapeDtypeStruct((batch, CONV_W - 1, D_GROUP), F32)],
        scratch_shapes=scratch,
        compiler_params=pltpu.CompilerParams(dimension_semantics=("arbitrary", "arbitrary"),
                                             vmem_limit_bytes=VMEM_LIMIT),
        name=f"mix_prompt_l{layer}",
    )(hp, *mixw)


def _mix_sample_kernel(h_ref, gm_ref, win_ref, lb_ref, hn_ref, cw_ref, cb_ref, lng_ref, lnb_ref, wout_ref,
                       s_ref, cs_ref, o_ref, s_out_ref, cs_out_ref,
                       q_s, f_s, k_s, v_s, g_s, glu_s, oraw_s, yc_s, blk_o, blk_y, *, layer):
    i = pl.program_id(0)
    sb = SAMPLE_BLOCK

    @pl.when(i == 0)
    def _():
        proj = _dot(_rms(h_ref[...], gm_ref[...]).astype(BF16), win_ref[...])
        part = lambda j: proj[:, j * D_GROUP:(j + 1) * D_GROUP]
        f, k = _gates(part(1), _lower_bound(lb_ref, layer))
        q_s[...] = _silu(part(0))
        f_s[...] = f
        k_s[...] = k
        v_s[...] = part(2)
        g_s[...] = _silu(part(3))
        glu_s[...] = part(4) * jax.nn.sigmoid(part(5))

    r0 = pl.multiple_of(i * sb, sb)
    rows = pl.ds(r0, sb)
    f8, k8, q8, v8, glu8 = f_s[rows, :], k_s[rows, :], q_s[rows, :], v_s[rows, :], glu_s[rows, :]
    pad = jnp.zeros((D_HEAD - 3 * sb, D_HEAD), F32)
    for h in range(N_HEADS):
        hs = slice(h * D_HEAD, (h + 1) * D_HEAD)
        cols = jnp.concatenate([f8[:, hs], k8[:, hs], q8[:, hs], pad], axis=0).T
        for j in range(sb):
            sn = cols[:, j:j + 1] * s_ref[j, h] + cols[:, sb + j:sb + j + 1] * v8[j:j + 1, hs]
            s_out_ref[j, h] = sn
            blk_o[j:j + 1, hs] = jnp.sum(sn * cols[:, 2 * sb + j:2 * sb + j + 1], axis=0, keepdims=True)
    for j in range(sb):
        glu_j = glu8[j:j + 1, :]
        blk_y[j:j + 1, :] = (jnp.sum(cs_ref[j] * cw_ref[0:CONV_W - 1, :], axis=0, keepdims=True)
                             + glu_j * cw_ref[CONV_W - 1:CONV_W, :])
        cs_out_ref[j, 0:CONV_W - 2, :] = cs_ref[j, 1:CONV_W - 1, :]
        cs_out_ref[j, CONV_W - 2:CONV_W - 1, :] = glu_j
    oraw_s[rows, :] = blk_o[...]
    yc_s[rows, :] = blk_y[...]

    @pl.when(i == pl.num_programs(0) - 1)
    def _():
        oraw = oraw_s[...]
        oa = jnp.concatenate(
            [_rms(oraw[:, h * D_HEAD:(h + 1) * D_HEAD], hn_ref[:, h * D_HEAD:(h + 1) * D_HEAD])
             for h in range(N_HEADS)], axis=-1) * g_s[...]
        ob = _layernorm_silu(yc_s[...] + cb_ref[...], lng_ref[...], lnb_ref[...])
        o_ref[...] = (h_ref[...] + _dot(oa.astype(BF16), wout_ref[0:D_GROUP, :])
                      + _dot(ob.astype(BF16), wout_ref[D_GROUP:2 * D_GROUP, :]))


def _mix_sample_call(hs, layer, state_hgrn, state_conv, mixw):
    n_s, d = hs.shape
    depth, _, d_in = mixw[1].shape
    sb = SAMPLE_BLOCK
    assert n_s % sb == 0 and 3 * sb <= D_HEAD
    whole = lambda i: (0, 0)
    scratch = [pltpu.VMEM((n_s, D_GROUP), F32) for _ in range(8)]
    scratch += [pltpu.VMEM((sb, D_GROUP), F32), pltpu.VMEM((sb, D_GROUP), F32)]
    return pl.pallas_call(
        functools.partial(_mix_sample_kernel, layer=layer),
        grid=(n_s // sb,),
        in_specs=[pl.BlockSpec((n_s, d), whole)] + _mixer_weight_specs(layer, depth, d, d_in)
        + [pl.BlockSpec((None, sb, N_HEADS, D_HEAD, D_HEAD), lambda i: (layer, i, 0, 0, 0)),
           pl.BlockSpec((None, sb, CONV_W - 1, D_GROUP), lambda i: (layer, i, 0, 0))],
        out_specs=[pl.BlockSpec((n_s, d), whole),
                   pl.BlockSpec((sb, N_HEADS, D_HEAD, D_HEAD), lambda i: (i, 0, 0, 0)),
                   pl.BlockSpec((sb, CONV_W - 1, D_GROUP), lambda i: (i, 0, 0))],
        out_shape=[jax.ShapeDtypeStruct(hs.shape, F32),
                   jax.ShapeDtypeStruct(state_hgrn.shape[1:], F32),
                   jax.ShapeDtypeStruct(state_conv.shape[1:], F32)],
        scratch_shapes=scratch,
        compiler_params=pltpu.CompilerParams(dimension_semantics=("arbitrary",), vmem_limit_bytes=VMEM_LIMIT),
        name=f"mix_sample_l{layer}",
    )(hs, *mixw, state_hgrn, state_conv)


def kernel(x_prompt, x_sample, state_hgrn, state_conv, p_prompt, p_sample, norm_ffn1, ffn1_w_gate, ffn1_w_up, ffn1_w_down, norm_mix, w_in, hgrn_lb, hgrn_norm, conv_w, conv_b, conv_ln_g, conv_ln_b, w_out, norm_ffn2, ffn2_w_gate, ffn2_w_up, ffn2_w_down, norm_ple, w_ple_gate, w_ple_in, norm_final):
    batch, seq, d = x_prompt.shape
    n_s = x_sample.shape[0] * x_sample.shape[1]
    depth = w_in.shape[0]
    assert w_in.shape[-1] == 6 * D_GROUP and hgrn_lb.shape == (depth, D_GROUP)
    assert state_hgrn.shape[2:] == (N_HEADS, D_HEAD, D_HEAD) and conv_w.shape[1:] == (CONV_W, D_GROUP)

    bf = lambda w: w.astype(BF16)
    row3 = lambda a: a.reshape(depth, 1, a.shape[-1])
    ffn1 = (row3(norm_ffn1), bf(ffn1_w_gate), bf(ffn1_w_up), bf(ffn1_w_down))
    ffn2 = (row3(norm_ffn2), bf(ffn2_w_gate), bf(ffn2_w_up), bf(ffn2_w_down))
    mixw = (row3(norm_mix), bf(w_in), hgrn_lb, row3(hgrn_norm), conv_w, row3(conv_b), row3(conv_ln_g),
            row3(conv_ln_b), bf(w_out))
    pp = p_prompt.reshape(depth, batch * seq, p_prompt.shape[-1])
    ps = p_sample.reshape(depth, n_s, p_sample.shape[-1])
    ple = (pp, ps, row3(norm_ple), bf(w_ple_gate), bf(w_ple_in))
    gain_final = norm_final.reshape(1, d)

    hp = x_prompt.reshape(batch * seq, d)
    hs = x_sample.reshape(n_s, d)
    s_p, c_p, s_s, c_s = [], [], [], []
    for layer in range(depth):
        hp, hs = _row_call(hp, hs, layer, *ffn1)
        hp, sp, cp = _mix_prompt_call(hp, layer, batch, seq, mixw)
        hs, ss, cs = _mix_sample_call(hs, layer, state_hgrn, state_conv, mixw)
        hp, hs = _row_call(hp, hs, layer, *ffn2, ple=ple, final_gain=gain_final if layer == depth - 1 else None)
        s_p.append(sp)
        c_p.append(cp)
        s_s.append(ss)
        c_s.append(cs)
    return (hp.reshape(x_prompt.shape), hs.reshape(x_sample.shape),
            jnp.stack(s_p), jnp.stack(c_p), jnp.stack(s_s), jnp.stack(c_s))
```

```python
import functools

import jax
import jax.numpy as jnp
from jax import lax
from jax.experimental import pallas as pl
from jax.experimental.pallas import tpu as pltpu

F32 = jnp.float32
BF16 = jnp.bfloat16

EPS = 1e-6
LN2 = 0.6931471805599453
N_HEADS = 4
D_HEAD = 128
MXU_WIDTH = 256
D_GROUP = 512
N_SLAB = D_GROUP // D_HEAD
CONV_W = 31
CHUNK = 64
SUB = 8
HIST = 32
PHASES = 4
CONV_BLOCK = PHASES * SUB

ROW_TILE = 512
MIX_TILE = 512
FF_SPLIT = 2
SAMPLE_BLOCK = 8
VMEM_LIMIT = 56 * 1024 * 1024


def _dot(a, b):
    return jnp.dot(a, b, preferred_element_type=F32)


def _dot_nt(a, b):
    return lax.dot_general(a, b, (((1,), (1,)), ((), ())), preferred_element_type=F32)


def _dot_tn(a, b):
    return lax.dot_general(a, b, (((0,), (0,)), ((), ())), preferred_element_type=F32)


def _rms(x, g):
    return x * lax.rsqrt(jnp.mean(x * x, axis=-1, keepdims=True) + EPS) * g


def _silu(x):
    return x * jax.nn.sigmoid(x)


def _lower_bound(lb_ref, layer):
    z = lb_ref[...]
    e = jnp.exp(z - jnp.max(z, axis=0, keepdims=True))
    sm = e / jnp.sum(e, axis=0, keepdims=True)
    c0 = sm[0:1, :]
    c = c0
    for i in range(1, layer + 1):
        c = c + sm[i:i + 1, :]
    return c - c0


def _gates(fr, lb):
    u = jnp.exp(-jnp.abs(fr))
    den = 1.0 / (1.0 + u)
    pos = fr >= 0.0
    f = jnp.where(pos, 1.0 + lb * u, u + lb) * den
    k = (1.0 - lb) * jnp.where(pos, u, 1.0) * den
    return f, k


def _layernorm_silu(y, g, b):
    mu = jnp.mean(y, axis=-1, keepdims=True)
    yc = y - mu
    return _silu(yc * lax.rsqrt(jnp.mean(yc * yc, axis=-1, keepdims=True) + EPS) * g + b)


def _row_kernel(*refs, n_prompt_tiles, d_ff, do_ple, final_norm):
    it = iter(refs)
    xp_ref, xs_ref = next(it), next(it)
    pp_ref, ps_ref = (next(it), next(it)) if do_ple else (None, None)
    g_ref, wg_ref, wu_ref, wd_ref = next(it), next(it), next(it), next(it)
    gp_ref, wpg_ref, wpi_ref = (next(it), next(it), next(it)) if do_ple else (None, None, None)
    gf_ref = next(it) if final_norm else None
    op_ref, os_ref = next(it), next(it)

    n_tiles = d_ff // MXU_WIDTH
    bounds = [((n_tiles * i) // FF_SPLIT) * MXU_WIDTH for i in range(FF_SPLIT)] + [d_ff]

    def process(x_ref, p_ref, o_ref):
        x = x_ref[...]
        xn = _rms(x, g_ref[...]).astype(BF16)
        acc = None
        for lo, hi in zip(bounds[:-1], bounds[1:]):
            gate = _dot(xn, wg_ref[:, lo:hi])
            up = _dot(xn, wu_ref[:, lo:hi])
            part = _dot((_silu(gate) * up).astype(BF16), wd_ref[lo:hi, :])
            acc = part if acc is None else acc + part
        h = x + 0.5 * acc
        if do_ple:
            gate = jax.nn.sigmoid(_dot(_rms(h, gp_ref[...]).astype(BF16), wpg_ref[...]))
            h = h + gate * _dot(p_ref[...].astype(BF16), wpi_ref[...])
        if final_norm:
            h = _rms(h, gf_ref[...])
        o_ref[...] = h

    i = pl.program_id(0)

    @pl.when(i < n_prompt_tiles)
    def _():
        process(xp_ref, pp_ref, op_ref)

    @pl.when(i == n_prompt_tiles)
    def _():
        process(xs_ref, ps_ref, os_ref)


def _resident(shape, index_map):
    return pl.BlockSpec(shape, index_map, pipeline_mode=pl.Buffered(1))


def _row_call(hp, hs, layer, g, wg, wu, wd, ple=None, final_gain=None):
    n_p, d = hp.shape
    n_s = hs.shape[0]
    d_ff = wg.shape[-1]
    assert n_p % ROW_TILE == 0 and d_ff % MXU_WIDTH == 0
    npt = n_p // ROW_TILE
    do_ple = ple is not None
    final_norm = final_gain is not None

    ptile = lambda i: (jnp.minimum(i, npt - 1), 0)
    whole = lambda i: (0, 0)
    lsel3 = lambda i: (layer, 0, 0)

    in_specs = [pl.BlockSpec((ROW_TILE, d), ptile), pl.BlockSpec((n_s, d), whole)]
    args = [hp, hs]
    if do_ple:
        pp, ps, gp, wpg, wpi = ple
        d_p = pp.shape[-1]
        in_specs += [pl.BlockSpec((None, ROW_TILE, d_p), lambda i: (layer, jnp.minimum(i, npt - 1), 0)),
                     pl.BlockSpec((None, n_s, d_p), lsel3)]
        args += [pp, ps]
    in_specs += [pl.BlockSpec((None, 1, d), lsel3),
                 _resident((None, d, d_ff), lsel3), _resident((None, d, d_ff), lsel3),
                 _resident((None, d_ff, d), lsel3)]
    args += [g, wg, wu, wd]
    if do_ple:
        in_specs += [pl.BlockSpec((None, 1, d), lsel3), _resident((None, d, d), lsel3),
                     _resident((None, d_p, d), lsel3)]
        args += [gp, wpg, wpi]
    if final_norm:
        in_specs += [pl.BlockSpec((1, d), whole)]
        args += [final_gain]

    return pl.pallas_call(
        functools.partial(_row_kernel, n_prompt_tiles=npt, d_ff=d_ff, do_ple=do_ple, final_norm=final_norm),
        grid=(npt + 1,),
        in_specs=in_specs,
        out_specs=[pl.BlockSpec((ROW_TILE, d), ptile), pl.BlockSpec((n_s, d), whole)],
        out_shape=[jax.ShapeDtypeStruct(hp.shape, F32), jax.ShapeDtypeStruct(hs.shape, F32)],
        compiler_params=pltpu.CompilerParams(dimension_semantics=("arbitrary",), vmem_limit_bytes=VMEM_LIMIT),
        name=f"rows_l{layer}_{'ple' if do_ple else 'ffn'}",
    )(*args)


N_LEVELS = CHUNK.bit_length() - 1


def _score_owner():
    t = lax.broadcasted_iota(jnp.int32, (CHUNK, CHUNK), 0)
    s = lax.broadcasted_iota(jnp.int32, (CHUNK, CHUNK), 1)
    return jnp.where(s > t, -1, jnp.where(s == t, N_LEVELS, 31 - lax.clz(t ^ s)))


def _hgrn_scores(qr, fr, lb, owner):
    nb = CHUNK // SUB
    row = lax.broadcasted_iota(jnp.int32, (SUB, D_HEAD), 0)
    bcast = lambda xb, r: jnp.broadcast_to(xb[r:r + 1, :], (SUB, D_HEAD))

    q = _silu(qr)
    f, k = _gates(fr, lb)
    lf = jnp.maximum(jnp.log(f), jnp.minimum(fr, 0.0) - LN2)

    g_blocks, ends = [], []
    carry = None
    for b in range(nb):
        xb = lf[SUB * b:SUB * (b + 1), :]
        for sft in (1, 2, 4):
            xb = xb + jnp.where(row >= sft, pltpu.roll(xb, sft, axis=0), 0.0)
        if carry is not None:
            xb = xb + carry
        carry = bcast(xb, SUB - 1)
        g_blocks.append(xb)
        ends.append(carry)
    g = jnp.concatenate(g_blocks, axis=0)

    refs = {}
    refs[2] = [jnp.where(row < 4, bcast(gb, 1), bcast(gb, 5)) for gb in g_blocks]
    refs[4] = [bcast(gb, 3) for gb in g_blocks]
    bs = SUB
    while bs < CHUNK:
        n = bs // SUB
        refs[bs] = [ends[(b // (2 * n)) * (2 * n) + n - 1] for b in range(nb)]
        bs *= 2

    odd = (lax.broadcasted_iota(jnp.int32, (CHUNK, D_HEAD), 0) & 1) == 1
    a = jnp.where(owner == 0, _dot_nt((q * jnp.where(odd, f, 1.0)).astype(BF16), k.astype(BF16)), 0.0)
    for level in range(1, N_LEVELS):
        e = jnp.exp(-jnp.abs(g - jnp.concatenate(refs[1 << level], axis=0)))
        a = jnp.where(owner == level, _dot_nt((q * e).astype(BF16), (k * e).astype(BF16)), a)
    a = jnp.where(owner == N_LEVELS, jnp.sum(q * k, axis=-1, keepdims=True), a)

    g_last = jnp.concatenate([ends[nb - 1]] * nb, axis=0)
    q_dec = q * jnp.exp(g)
    k_dec = k * jnp.exp(g_last - g)
    return a.astype(BF16), q_dec.astype(BF16), k_dec.astype(BF16), jnp.exp(ends[nb - 1])


def _mix_prompt_kernel(h_ref, gm_ref, win_ref, lb_ref, hn_ref, cw_ref, cb_ref, lng_ref, lnb_ref, wout_ref,
                       *rest, layer, tt, n_prev):
    (o_ref, s_out_ref, c_out_ref,
     proj_s, oa_s, yc_s, gbuf, st_s, own_s, a_s, qd_s, kd_s, dec_s) = rest[n_prev:]
    t = pl.program_id(1)

    @pl.when(t == 0)
    def _():
        st_s[...] = jnp.zeros_like(st_s)
        gbuf[:, 0:HIST, :] = jnp.zeros((N_SLAB, HIST, D_HEAD), F32)

    own_s[...] = _score_owner()
    proj_s[...] = _dot(_rms(h_ref[...], gm_ref[...]).astype(BF16), win_ref[...])
    lb = _lower_bound(lb_ref, layer)
    col = lambda j, h: slice(j * D_GROUP + h * D_HEAD, j * D_GROUP + (h + 1) * D_HEAD)
    chunk_rows = lambda c: pl.ds(pl.multiple_of(c * CHUNK, CHUNK), CHUNK)

    for j in range(N_SLAB):
        gbuf[j, HIST:HIST + tt, :] = proj_s[:, col(4, j)] * jax.nn.sigmoid(proj_s[:, col(5, j)])

    def conv_body(i, carry):
        base = pl.multiple_of(i * CONV_BLOCK, CONV_BLOCK)
        for j in range(N_SLAB):
            js = slice(j * D_HEAD, (j + 1) * D_HEAD)
            accs = [jnp.broadcast_to(cb_ref[:, js], (SUB, D_HEAD)) for _ in range(PHASES)]
            for w in range(CONV_W):
                cw = cw_ref[w:w + 1, js]
                for p in range(PHASES):
                    start = base + (HIST - (CONV_W - 1) + p + w)
                    accs[p] = accs[p] + gbuf[j, pl.ds(start, SUB, stride=PHASES), :] * cw
            for p in range(PHASES):
                yc_s[j, pl.ds(base + p, SUB, stride=PHASES), :] = accs[p]
        return carry

    lax.fori_loop(0, tt // CONV_BLOCK, conv_body, 0)

    def stash_scores(c):
        rows = chunk_rows(c)
        owner = own_s[...]
        for h in range(N_HEADS):
            hs = slice(h * D_HEAD, (h + 1) * D_HEAD)
            a_s[h], qd_s[h], kd_s[h], dec_s[h] = _hgrn_scores(proj_s[rows, col(0, h)], proj_s[rows, col(1, h)],
                                                               lb[:, hs], owner)

    def advance_state(c):
        rows = chunk_rows(c)
        for h in range(N_HEADS):
            hs = slice(h * D_HEAD, (h + 1) * D_HEAD)
            v = proj_s[rows, col(2, h)].astype(BF16)
            st = st_s[h]
            o = _dot_nt(qd_s[h], st.astype(BF16)) + _dot(a_s[h], v)
            st_s[h] = st * dec_s[h][0:1, :] + _dot_tn(v, kd_s[h])
            oa_s[rows, hs] = _rms(o, hn_ref[:, hs]) * _silu(proj_s[rows, col(3, h)])

    def chunk_body(c, carry):
        advance_state(c - 1)
        stash_scores(c)
        return carry

    n_chunks = tt // CHUNK
    stash_scores(0)
    lax.fori_loop(1, n_chunks, chunk_body, 0)
    advance_state(n_chunks - 1)

    ob = _layernorm_silu(jnp.concatenate([yc_s[j] for j in range(N_SLAB)], axis=-1), lng_ref[...], lnb_ref[...])
    o_ref[...] = (h_ref[...] + _dot(oa_s[...].astype(BF16), wout_ref[0:D_GROUP, :])
                  + _dot(ob.astype(BF16), wout_ref[D_GROUP:2 * D_GROUP, :]))

    @pl.when(t == pl.num_programs(1) - 1)
    def _():
        lo = tt + HIST - (CONV_W - 1)
        c_out_ref[...] = jnp.concatenate([gbuf[j, lo:lo + CONV_W - 1, :] for j in range(N_SLAB)], axis=-1)
        for h in range(N_HEADS):
            s_out_ref[h] = st_s[h].T

    gbuf[:, 0:HIST, :] = gbuf[:, tt:tt + HIST, :]


def _mixer_weight_specs(layer, depth, d, d_in):
    lsel3 = lambda *_: (layer, 0, 0)
    whole2 = lambda *_: (0, 0)
    return [pl.BlockSpec((None, 1, d), lsel3),
            _resident((None, d, d_in), lsel3),
            pl.BlockSpec((depth, D_GROUP), whole2),
            pl.BlockSpec((None, 1, D_GROUP), lsel3),
            pl.BlockSpec((None, CONV_W, D_GROUP), lsel3),
            pl.BlockSpec((None, 1, D_GROUP), lsel3),
            pl.BlockSpec((None, 1, D_GROUP), lsel3),
            pl.BlockSpec((None, 1, D_GROUP), lsel3),
            _resident((None, 2 * D_GROUP, d), lsel3)]


def _stacked_state_outputs(prev, n_args):
    if prev is None:
        return [], [], {}
    return ([pl.BlockSpec(memory_space=pl.ANY)] * len(prev), list(prev),
            {n_args + i: 1 + i for i in range(len(prev))})


def _mix_prompt_call(hp, layer, batch, seq, mixw, prev):
    n_p, d = hp.shape
    depth, _, d_in = mixw[1].shape
    tt = MIX_TILE
    assert seq % tt == 0 and tt % CHUNK == 0 and tt % CONV_BLOCK == 0 and tt >= HIST
    nt = seq // tt
    tile = lambda b, t: (b * nt + t, 0)
    prev_specs, prev_args, aliases = _stacked_state_outputs(prev, 1 + len(mixw))
    scratch = [pltpu.VMEM((tt, d_in), F32),
               pltpu.VMEM((tt, D_GROUP), F32),
               pltpu.VMEM((N_SLAB, tt, D_HEAD), F32),
               pltpu.VMEM((N_SLAB, tt + HIST, D_HEAD), F32),
               pltpu.VMEM((N_HEADS, D_HEAD, D_HEAD), F32),
               pltpu.VMEM((CHUNK, CHUNK), jnp.int32),
               pltpu.VMEM((N_HEADS, CHUNK, CHUNK), BF16),
               pltpu.VMEM((N_HEADS, CHUNK, D_HEAD), BF16),
               pltpu.VMEM((N_HEADS, CHUNK, D_HEAD), BF16),
               pltpu.VMEM((N_HEADS, SUB, D_HEAD), F32)]
    return pl.pallas_call(
        functools.partial(_mix_prompt_kernel, layer=layer, tt=tt, n_prev=len(prev_args)),
        grid=(batch, nt),
        in_specs=[pl.BlockSpec((tt, d), tile)] + _mixer_weight_specs(layer, depth, d, d_in) + prev_specs,
        out_specs=[pl.BlockSpec((tt, d), tile),
                   pl.BlockSpec((None, None, N_HEADS, D_HEAD, D_HEAD), lambda b, t: (layer, b, 0, 0, 0)),
                   pl.BlockSpec((None, None, CONV_W - 1, D_GROUP), lambda b, t: (layer, b, 0, 0))],
        out_shape=[jax.ShapeDtypeStruct(hp.shape, F32),
                   jax.ShapeDtypeStruct((depth, batch, N_HEADS, D_HEAD, D_HEAD), F32),
                   jax.ShapeDtypeStruct((depth, batch, CONV_W - 1, D_GROUP), F32)],
        scratch_shapes=scratch,
        input_output_aliases=aliases,
        compiler_params=pltpu.CompilerParams(dimension_semantics=("arbitrary", "arbitrary"),
                                             vmem_limit_bytes=VMEM_LIMIT),
        name=f"mix_prompt_l{layer}",
    )(hp, *mixw, *prev_args)


def _mix_sample_kernel(h_ref, gm_ref, win_ref, lb_ref, hn_ref, cw_ref, cb_ref, lng_ref, lnb_ref, wout_ref,
                       s_ref, cs_ref, *rest, layer, n_prev):
    (o_ref, s_out_ref, cs_out_ref,
     q_s, f_s, k_s, v_s, g_s, glu_s, oraw_s, yc_s, blk_o, blk_y) = rest[n_prev:]
    i = pl.program_id(0)
    sb = SAMPLE_BLOCK

    @pl.when(i == 0)
    def _():
        proj = _dot(_rms(h_ref[...], gm_ref[...]).astype(BF16), win_ref[...])
        part = lambda j: proj[:, j * D_GROUP:(j + 1) * D_GROUP]
        f, k = _gates(part(1), _lower_bound(lb_ref, layer))
        q_s[...] = _silu(part(0))
        f_s[...] = f
        k_s[...] = k
        v_s[...] = part(2)
        g_s[...] = _silu(part(3))
        glu_s[...] = part(4) * jax.nn.sigmoid(part(5))

    r0 = pl.multiple_of(i * sb, sb)
    rows = pl.ds(r0, sb)
    f8, k8, q8, v8, glu8 = f_s[rows, :], k_s[rows, :], q_s[rows, :], v_s[rows, :], glu_s[rows, :]
    pad = jnp.zeros((D_HEAD - 3 * sb, D_HEAD), F32)
    for h in range(N_HEADS):
        hs = slice(h * D_HEAD, (h + 1) * D_HEAD)
        cols = jnp.concatenate([f8[:, hs], k8[:, hs], q8[:, hs], pad], axis=0).T
        for j in range(sb):
            sn = cols[:, j:j + 1] * s_ref[j, h] + cols[:, sb + j:sb + j + 1] * v8[j:j + 1, hs]
            s_out_ref[j, h] = sn
            blk_o[j:j + 1, hs] = jnp.sum(sn * cols[:, 2 * sb + j:2 * sb + j + 1], axis=0, keepdims=True)
    for j in range(sb):
        glu_j = glu8[j:j + 1, :]
        blk_y[j:j + 1, :] = (jnp.sum(cs_ref[j] * cw_ref[0:CONV_W - 1, :], axis=0, keepdims=True)
                             + glu_j * cw_ref[CONV_W - 1:CONV_W, :])
        cs_out_ref[j, 0:CONV_W - 2, :] = cs_ref[j, 1:CONV_W - 1, :]
        cs_out_ref[j, CONV_W - 2:CONV_W - 1, :] = glu_j
    oraw_s[rows, :] = blk_o[...]
    yc_s[rows, :] = blk_y[...]

    @pl.when(i == pl.num_programs(0) - 1)
    def _():
        oraw = oraw_s[...]
        oa = jnp.concatenate(
            [_rms(oraw[:, h * D_HEAD:(h + 1) * D_HEAD], hn_ref[:, h * D_HEAD:(h + 1) * D_HEAD])
             for h in range(N_HEADS)], axis=-1) * g_s[...]
        ob = _layernorm_silu(yc_s[...] + cb_ref[...], lng_ref[...], lnb_ref[...])
        o_ref[...] = (h_ref[...] + _dot(oa.astype(BF16), wout_ref[0:D_GROUP, :])
                      + _dot(ob.astype(BF16), wout_ref[D_GROUP:2 * D_GROUP, :]))


def _mix_sample_call(hs, layer, state_hgrn, state_conv, mixw, prev):
    n_s, d = hs.shape
    depth, _, d_in = mixw[1].shape
    sb = SAMPLE_BLOCK
    assert n_s % sb == 0 and 3 * sb <= D_HEAD
    whole = lambda i: (0, 0)
    s_spec = pl.BlockSpec((None, sb, N_HEADS, D_HEAD, D_HEAD), lambda i: (layer, i, 0, 0, 0))
    cs_spec = pl.BlockSpec((None, sb, CONV_W - 1, D_GROUP), lambda i: (layer, i, 0, 0))
    prev_specs, prev_args, aliases = _stacked_state_outputs(prev, 3 + len(mixw))
    scratch = [pltpu.VMEM((n_s, D_GROUP), F32) for _ in range(8)]
    scratch += [pltpu.VMEM((sb, D_GROUP), F32), pltpu.VMEM((sb, D_GROUP), F32)]
    return pl.pallas_call(
        functools.partial(_mix_sample_kernel, layer=layer, n_prev=len(prev_args)),
        grid=(n_s // sb,),
        in_specs=[pl.BlockSpec((n_s, d), whole)] + _mixer_weight_specs(layer, depth, d, d_in)
        + [s_spec, cs_spec] + prev_specs,
        out_specs=[pl.BlockSpec((n_s, d), whole), s_spec, cs_spec],
        out_shape=[jax.ShapeDtypeStruct(hs.shape, F32),
                   jax.ShapeDtypeStruct(state_hgrn.shape, F32),
                   jax.ShapeDtypeStruct(state_conv.shape, F32)],
        scratch_shapes=scratch,
        input_output_aliases=aliases,
        compiler_params=pltpu.CompilerParams(dimension_semantics=("arbitrary",), vmem_limit_bytes=VMEM_LIMIT),
        name=f"mix_sample_l{layer}",
    )(hs, *mixw, state_hgrn, state_conv, *prev_args)


def kernel(x_prompt, x_sample, state_hgrn, state_conv, p_prompt, p_sample, norm_ffn1, ffn1_w_gate, ffn1_w_up, ffn1_w_down, norm_mix, w_in, hgrn_lb, hgrn_norm, conv_w, conv_b, conv_ln_g, conv_ln_b, w_out, norm_ffn2, ffn2_w_gate, ffn2_w_up, ffn2_w_down, norm_ple, w_ple_gate, w_ple_in, norm_final):
    batch, seq, d = x_prompt.shape
    n_s = x_sample.shape[0] * x_sample.shape[1]
    depth = w_in.shape[0]
    assert w_in.shape[-1] == 6 * D_GROUP and hgrn_lb.shape == (depth, D_GROUP)
    assert state_hgrn.shape[2:] == (N_HEADS, D_HEAD, D_HEAD) and conv_w.shape[1:] == (CONV_W, D_GROUP)

    bf = lambda w: w.astype(BF16)
    row3 = lambda a: a.reshape(depth, 1, a.shape[-1])
    ffn1 = (row3(norm_ffn1), bf(ffn1_w_gate), bf(ffn1_w_up), bf(ffn1_w_down))
    ffn2 = (row3(norm_ffn2), bf(ffn2_w_gate), bf(ffn2_w_up), bf(ffn2_w_down))
    mixw = (row3(norm_mix), bf(w_in), hgrn_lb, row3(hgrn_norm), conv_w, row3(conv_b), row3(conv_ln_g),
            row3(conv_ln_b), bf(w_out))
    pp = p_prompt.reshape(depth, batch * seq, p_prompt.shape[-1])
    ps = p_sample.reshape(depth, n_s, p_sample.shape[-1])
    ple = (pp, ps, row3(norm_ple), bf(w_ple_gate), bf(w_ple_in))
    gain_final = norm_final.reshape(1, d)

    hp = x_prompt.reshape(batch * seq, d)
    hs = x_sample.reshape(n_s, d)
    new_p = new_s = None
    for layer in range(depth):
        hp, hs = _row_call(hp, hs, layer, *ffn1)
        hp, *new_p = _mix_prompt_call(hp, layer, batch, seq, mixw, new_p)
        hs, *new_s = _mix_sample_call(hs, layer, state_hgrn, state_conv, mixw, new_s)
        hp, hs = _row_call(hp, hs, layer, *ffn2, ple=ple, final_gain=gain_final if layer == depth - 1 else None)
    return (hp.reshape(x_prompt.shape), hs.reshape(x_sample.shape), new_p[0], new_p[1], new_s[0], new_s[1])
```

```python
import functools

import jax
import jax.numpy as jnp
from jax import lax
from jax.experimental import pallas as pl
from jax.experimental.pallas import tpu as pltpu

F32 = jnp.float32
BF16 = jnp.bfloat16

EPS = 1e-6
LN2 = 0.6931471805599453
N_HEADS = 4
D_HEAD = 128
MXU_WIDTH = 256
D_GROUP = 512
N_SLAB = D_GROUP // D_HEAD
CONV_W = 31
CHUNK = 64
SUB = 8
HIST = 32
PHASES = 4
CONV_BLOCK = PHASES * SUB

ROW_TILE = 512
MIX_TILE = 512
FF_SPLIT = 2
SAMPLE_BLOCK = 16
VMEM_LIMIT = 56 * 1024 * 1024


def _dot(a, b):
    return jnp.dot(a, b, preferred_element_type=F32)


def _dot_nt(a, b):
    return lax.dot_general(a, b, (((1,), (1,)), ((), ())), preferred_element_type=F32)


def _dot_tn(a, b):
    return lax.dot_general(a, b, (((0,), (0,)), ((), ())), preferred_element_type=F32)


def _rms(x, g):
    return x * lax.rsqrt(jnp.mean(x * x, axis=-1, keepdims=True) + EPS) * g


def _silu(x):
    return x * jax.nn.sigmoid(x)


def _lower_bound(lb_ref, layer):
    z = lb_ref[...]
    e = jnp.exp(z - jnp.max(z, axis=0, keepdims=True))
    sm = e / jnp.sum(e, axis=0, keepdims=True)
    c0 = sm[0:1, :]
    c = c0
    for i in range(1, layer + 1):
        c = c + sm[i:i + 1, :]
    return c - c0


def _gates(fr, lb):
    u = jnp.exp(-jnp.abs(fr))
    den = 1.0 / (1.0 + u)
    pos = fr >= 0.0
    f = jnp.where(pos, 1.0 + lb * u, u + lb) * den
    k = (1.0 - lb) * jnp.where(pos, u, 1.0) * den
    return f, k


def _layernorm_silu(y, g, b):
    mu = jnp.mean(y, axis=-1, keepdims=True)
    yc = y - mu
    return _silu(yc * lax.rsqrt(jnp.mean(yc * yc, axis=-1, keepdims=True) + EPS) * g + b)


def _row_kernel(*refs, n_prompt_tiles, d_ff, do_ple, final_norm):
    it = iter(refs)
    xp_ref, xs_ref = next(it), next(it)
    pp_ref, ps_ref = (next(it), next(it)) if do_ple else (None, None)
    g_ref, wg_ref, wu_ref, wd_ref = next(it), next(it), next(it), next(it)
    gp_ref, wpg_ref, wpi_ref = (next(it), next(it), next(it)) if do_ple else (None, None, None)
    gf_ref = next(it) if final_norm else None
    op_ref, os_ref = next(it), next(it)

    n_tiles = d_ff // MXU_WIDTH
    bounds = [((n_tiles * i) // FF_SPLIT) * MXU_WIDTH for i in range(FF_SPLIT)] + [d_ff]

    def process(x_ref, p_ref, o_ref):
        x = x_ref[...]
        xn = _rms(x, g_ref[...]).astype(BF16)
        acc = None
        for lo, hi in zip(bounds[:-1], bounds[1:]):
            gate = _dot(xn, wg_ref[:, lo:hi])
            up = _dot(xn, wu_ref[:, lo:hi])
            part = _dot((_silu(gate) * up).astype(BF16), wd_ref[lo:hi, :])
            acc = part if acc is None else acc + part
        h = x + 0.5 * acc
        if do_ple:
            gate = jax.nn.sigmoid(_dot(_rms(h, gp_ref[...]).astype(BF16), wpg_ref[...]))
            h = h + gate * _dot(p_ref[...].astype(BF16), wpi_ref[...])
        if final_norm:
            h = _rms(h, gf_ref[...])
        o_ref[...] = h

    i = pl.program_id(0)

    @pl.when(i < n_prompt_tiles)
    def _():
        process(xp_ref, pp_ref, op_ref)

    @pl.when(i == n_prompt_tiles)
    def _():
        process(xs_ref, ps_ref, os_ref)


def _resident(shape, index_map):
    return pl.BlockSpec(shape, index_map, pipeline_mode=pl.Buffered(1))


def _row_call(hp, hs, layer, g, wg, wu, wd, ple=None, final_gain=None):
    n_p, d = hp.shape
    n_s = hs.shape[0]
    d_ff = wg.shape[-1]
    assert n_p % ROW_TILE == 0 and d_ff % MXU_WIDTH == 0
    npt = n_p // ROW_TILE
    do_ple = ple is not None
    final_norm = final_gain is not None

    ptile = lambda i: (jnp.minimum(i, npt - 1), 0)
    whole = lambda i: (0, 0)
    lsel3 = lambda i: (layer, 0, 0)

    in_specs = [pl.BlockSpec((ROW_TILE, d), ptile), pl.BlockSpec((n_s, d), whole)]
    args = [hp, hs]
    if do_ple:
        pp, ps, gp, wpg, wpi = ple
        d_p = pp.shape[-1]
        in_specs += [pl.BlockSpec((None, ROW_TILE, d_p), lambda i: (layer, jnp.minimum(i, npt - 1), 0)),
                     pl.BlockSpec((None, n_s, d_p), lsel3)]
        args += [pp, ps]
    in_specs += [pl.BlockSpec((None, 1, d), lsel3),
                 _resident((None, d, d_ff), lsel3), _resident((None, d, d_ff), lsel3),
                 _resident((None, d_ff, d), lsel3)]
    args += [g, wg, wu, wd]
    if do_ple:
        in_specs += [pl.BlockSpec((None, 1, d), lsel3), _resident((None, d, d), lsel3),
                     _resident((None, d_p, d), lsel3)]
        args += [gp, wpg, wpi]
    if final_norm:
        in_specs += [pl.BlockSpec((1, d), whole)]
        args += [final_gain]

    return pl.pallas_call(
        functools.partial(_row_kernel, n_prompt_tiles=npt, d_ff=d_ff, do_ple=do_ple, final_norm=final_norm),
        grid=(npt + 1,),
        in_specs=in_specs,
        out_specs=[pl.BlockSpec((ROW_TILE, d), ptile), pl.BlockSpec((n_s, d), whole)],
        out_shape=[jax.ShapeDtypeStruct(hp.shape, F32), jax.ShapeDtypeStruct(hs.shape, F32)],
        compiler_params=pltpu.CompilerParams(dimension_semantics=("arbitrary",), vmem_limit_bytes=VMEM_LIMIT),
        name=f"rows_l{layer}_{'ple' if do_ple else 'ffn'}",
    )(*args)


N_LEVELS = CHUNK.bit_length() - 1


def _score_owner():
    t = lax.broadcasted_iota(jnp.int32, (CHUNK, CHUNK), 0)
    s = lax.broadcasted_iota(jnp.int32, (CHUNK, CHUNK), 1)
    return jnp.where(s > t, -1, jnp.where(s == t, N_LEVELS, 31 - lax.clz(t ^ s)))


def _hgrn_scores(q, k, lf, q_adj, owner):
    nb = CHUNK // SUB
    row = lax.broadcasted_iota(jnp.int32, (SUB, D_HEAD), 0)
    bcast = lambda xb, r: jnp.broadcast_to(xb[r:r + 1, :], (SUB, D_HEAD))

    g_blocks, ends = [], []
    carry = None
    for b in range(nb):
        xb = lf[SUB * b:SUB * (b + 1), :]
        for sft in (1, 2, 4):
            xb = xb + jnp.where(row >= sft, pltpu.roll(xb, sft, axis=0), 0.0)
        if carry is not None:
            xb = xb + carry
        carry = bcast(xb, SUB - 1)
        g_blocks.append(xb)
        ends.append(carry)
    g = jnp.concatenate(g_blocks, axis=0)

    def level_factor(level):
        bs = 1 << level
        if bs == 2:
            return [jnp.exp(-jnp.abs(gb - jnp.where(row < 4, bcast(gb, 1), bcast(gb, 5)))) for gb in g_blocks]
        if bs == 4:
            return [jnp.exp(-jnp.abs(gb - bcast(gb, 3))) for gb in g_blocks]
        n = bs // SUB
        out = []
        for b, gb in enumerate(g_blocks):
            ref = ends[(b // (2 * n)) * (2 * n) + n - 1]
            out.append(jnp.exp(gb - ref if (b // n) % 2 == 1 else ref - gb))
        return out

    a = jnp.where(owner == 0, _dot_nt(q_adj, k.astype(BF16)), 0.0)
    for level in range(1, N_LEVELS):
        e = jnp.concatenate(level_factor(level), axis=0)
        a = jnp.where(owner == level, _dot_nt((q * e).astype(BF16), (k * e).astype(BF16)), a)
    a = jnp.where(owner == N_LEVELS, jnp.sum(q * k, axis=-1, keepdims=True), a)

    g_last = jnp.concatenate([ends[nb - 1]] * nb, axis=0)
    q_dec = q * jnp.exp(g)
    k_dec = k * jnp.exp(g_last - g)
    return a.astype(BF16), q_dec.astype(BF16), k_dec.astype(BF16), jnp.exp(ends[nb - 1])


def _mix_prompt_kernel(h_ref, gm_ref, win_ref, lb_ref, hn_ref, cw_ref, cb_ref, lng_ref, lnb_ref, wout_ref,
                       *rest, layer, tt, n_prev):
    (o_ref, s_out_ref, c_out_ref,
     q_s, k_s, lf_s, qa_s, v_s, gs_s, oa_s, yc_s, gbuf, st_s, own_s, a_s, qd_s, kd_s, dec_s) = rest[n_prev:]
    t = pl.program_id(1)

    @pl.when(t == 0)
    def _():
        st_s[...] = jnp.zeros_like(st_s)
        gbuf[:, 0:HIST, :] = jnp.zeros((N_SLAB, HIST, D_HEAD), F32)

    own_s[...] = _score_owner()
    chunk_rows = lambda c: pl.ds(pl.multiple_of(c * CHUNK, CHUNK), CHUNK)

    xn = _rms(h_ref[...], gm_ref[...]).astype(BF16)
    group = lambda j: _dot(xn, win_ref[:, j * D_GROUP:(j + 1) * D_GROUP])
    q = _silu(group(0))
    fr = group(1)
    f, k = _gates(fr, _lower_bound(lb_ref, layer))
    odd = (lax.broadcasted_iota(jnp.int32, (tt, D_GROUP), 0) & 1) == 1
    q_s[...] = q
    qa_s[...] = (q * jnp.where(odd, f, 1.0)).astype(BF16)
    k_s[...] = k
    lf_s[...] = jnp.maximum(jnp.log(f), jnp.minimum(fr, 0.0) - LN2)
    v_s[...] = group(2).astype(BF16)
    gs_s[...] = _silu(group(3))
    glu = group(4) * jax.nn.sigmoid(group(5))
    for j in range(N_SLAB):
        gbuf[j, HIST:HIST + tt, :] = glu[:, j * D_HEAD:(j + 1) * D_HEAD]

    def conv_block(i):
        base = pl.multiple_of(i * CONV_BLOCK, CONV_BLOCK)
        for j in range(N_SLAB):
            js = slice(j * D_HEAD, (j + 1) * D_HEAD)
            accs = [jnp.broadcast_to(cb_ref[:, js], (SUB, D_HEAD)) for _ in range(PHASES)]
            for w in range(CONV_W):
                cw = cw_ref[w:w + 1, js]
                for p in range(PHASES):
                    start = base + (HIST - (CONV_W - 1) + p + w)
                    accs[p] = accs[p] + gbuf[j, pl.ds(start, SUB, stride=PHASES), :] * cw
            for p in range(PHASES):
                yc_s[j, pl.ds(base + p, SUB, stride=PHASES), :] = accs[p]

    def stash_scores(c):
        rows = chunk_rows(c)
        owner = own_s[...]
        for h in range(N_HEADS):
            hs = slice(h * D_HEAD, (h + 1) * D_HEAD)
            a_s[h], qd_s[h], kd_s[h], dec_s[h] = _hgrn_scores(q_s[rows, hs], k_s[rows, hs], lf_s[rows, hs],
                                                               qa_s[rows, hs], owner)

    def advance_state(c):
        rows = chunk_rows(c)
        for h in range(N_HEADS):
            hs = slice(h * D_HEAD, (h + 1) * D_HEAD)
            v = v_s[rows, hs]
            st = st_s[h]
            o = _dot_nt(qd_s[h], st.astype(BF16)) + _dot(a_s[h], v)
            st_s[h] = st * dec_s[h][0:1, :] + _dot_tn(v, kd_s[h])
            oa_s[rows, hs] = _rms(o, hn_ref[:, hs]) * gs_s[rows, hs]

    n_chunks = tt // CHUNK
    conv_per_chunk = tt // CONV_BLOCK // n_chunks

    def chunk_body(c, carry):
        advance_state(c - 1)
        stash_scores(c)
        for r in range(conv_per_chunk):
            conv_block(c * conv_per_chunk + r)
        return carry

    stash_scores(0)
    for r in range(conv_per_chunk):
        conv_block(r)
    lax.fori_loop(1, n_chunks, chunk_body, 0)
    advance_state(n_chunks - 1)

    ob = _layernorm_silu(jnp.concatenate([yc_s[j] for j in range(N_SLAB)], axis=-1), lng_ref[...], lnb_ref[...])
    o_ref[...] = (h_ref[...] + _dot(oa_s[...].astype(BF16), wout_ref[0:D_GROUP, :])
                  + _dot(ob.astype(BF16), wout_ref[D_GROUP:2 * D_GROUP, :]))

    @pl.when(t == pl.num_programs(1) - 1)
    def _():
        lo = tt + HIST - (CONV_W - 1)
        c_out_ref[...] = jnp.concatenate([gbuf[j, lo:lo + CONV_W - 1, :] for j in range(N_SLAB)], axis=-1)
        for h in range(N_HEADS):
            s_out_ref[h] = st_s[h].T

    gbuf[:, 0:HIST, :] = gbuf[:, tt:tt + HIST, :]


def _mixer_weight_specs(layer, depth, d, d_in):
    lsel3 = lambda *_: (layer, 0, 0)
    whole2 = lambda *_: (0, 0)
    return [pl.BlockSpec((None, 1, d), lsel3),
            _resident((None, d, d_in), lsel3),
            pl.BlockSpec((depth, D_GROUP), whole2),
            pl.BlockSpec((None, 1, D_GROUP), lsel3),
            pl.BlockSpec((None, CONV_W, D_GROUP), lsel3),
            pl.BlockSpec((None, 1, D_GROUP), lsel3),
            pl.BlockSpec((None, 1, D_GROUP), lsel3),
            pl.BlockSpec((None, 1, D_GROUP), lsel3),
            _resident((None, 2 * D_GROUP, d), lsel3)]


def _stacked_state_outputs(prev, n_args):
    if prev is None:
        return [], [], {}
    return ([pl.BlockSpec(memory_space=pl.ANY)] * len(prev), list(prev),
            {n_args + i: 1 + i for i in range(len(prev))})


def _mix_prompt_call(hp, layer, batch, seq, mixw, prev):
    n_p, d = hp.shape
    depth, _, d_in = mixw[1].shape
    tt = MIX_TILE
    assert seq % tt == 0 and tt % CHUNK == 0 and tt % CONV_BLOCK == 0 and tt >= HIST
    nt = seq // tt
    tile = lambda b, t: (b * nt + t, 0)
    prev_specs, prev_args, aliases = _stacked_state_outputs(prev, 1 + len(mixw))
    scratch = [pltpu.VMEM((tt, D_GROUP), F32),
               pltpu.VMEM((tt, D_GROUP), F32),
               pltpu.VMEM((tt, D_GROUP), F32),
               pltpu.VMEM((tt, D_GROUP), BF16),
               pltpu.VMEM((tt, D_GROUP), BF16),
               pltpu.VMEM((tt, D_GROUP), F32),
               pltpu.VMEM((tt, D_GROUP), F32),
               pltpu.VMEM((N_SLAB, tt, D_HEAD), F32),
               pltpu.VMEM((N_SLAB, tt + HIST, D_HEAD), F32),
               pltpu.VMEM((N_HEADS, D_HEAD, D_HEAD), F32),
               pltpu.VMEM((CHUNK, CHUNK), jnp.int32),
               pltpu.VMEM((N_HEADS, CHUNK, CHUNK), BF16),
               pltpu.VMEM((N_HEADS, CHUNK, D_HEAD), BF16),
               pltpu.VMEM((N_HEADS, CHUNK, D_HEAD), BF16),
               pltpu.VMEM((N_HEADS, SUB, D_HEAD), F32)]
    return pl.pallas_call(
        functools.partial(_mix_prompt_kernel, layer=layer, tt=tt, n_prev=len(prev_args)),
        grid=(batch, nt),
        in_specs=[pl.BlockSpec((tt, d), tile)] + _mixer_weight_specs(layer, depth, d, d_in) + prev_specs,
        out_specs=[pl.BlockSpec((tt, d), tile),
                   pl.BlockSpec((None, None, N_HEADS, D_HEAD, D_HEAD), lambda b, t: (layer, b, 0, 0, 0)),
                   pl.BlockSpec((None, None, CONV_W - 1, D_GROUP), lambda b, t: (layer, b, 0, 0))],
        out_shape=[jax.ShapeDtypeStruct(hp.shape, F32),
                   jax.ShapeDtypeStruct((depth, batch, N_HEADS, D_HEAD, D_HEAD), F32),
                   jax.ShapeDtypeStruct((depth, batch, CONV_W - 1, D_GROUP), F32)],
        scratch_shapes=scratch,
        input_output_aliases=aliases,
        compiler_params=pltpu.CompilerParams(dimension_semantics=("arbitrary", "arbitrary"),
                                             vmem_limit_bytes=VMEM_LIMIT),
        name=f"mix_prompt_l{layer}",
    )(hp, *mixw, *prev_args)


def _mix_sample_kernel(h_ref, gm_ref, win_ref, lb_ref, hn_ref, cw_ref, cb_ref, lng_ref, lnb_ref, wout_ref,
                       s_ref, cs_ref, *rest, layer, n_prev):
    (o_ref, s_out_ref, cs_out_ref,
     q_s, f_s, k_s, v_s, g_s, glu_s, oraw_s, yc_s, blk_o, blk_y) = rest[n_prev:]
    i = pl.program_id(0)
    sb = SAMPLE_BLOCK

    @pl.when(i == 0)
    def _():
        proj = _dot(_rms(h_ref[...], gm_ref[...]).astype(BF16), win_ref[...])
        part = lambda j: proj[:, j * D_GROUP:(j + 1) * D_GROUP]
        f, k = _gates(part(1), _lower_bound(lb_ref, layer))
        q_s[...] = _silu(part(0))
        f_s[...] = f
        k_s[...] = k
        v_s[...] = part(2)
        g_s[...] = _silu(part(3))
        glu_s[...] = part(4) * jax.nn.sigmoid(part(5))

    r0 = pl.multiple_of(i * sb, sb)
    rows = pl.ds(r0, sb)
    f8, k8, q8, v8, glu8 = f_s[rows, :], k_s[rows, :], q_s[rows, :], v_s[rows, :], glu_s[rows, :]
    pad = jnp.zeros((D_HEAD - 3 * sb, D_HEAD), F32)
    for h in range(N_HEADS):
        hs = slice(h * D_HEAD, (h + 1) * D_HEAD)
        cols = jnp.concatenate([f8[:, hs], k8[:, hs], q8[:, hs], pad], axis=0).T
        for j in range(sb):
            sn = cols[:, j:j + 1] * s_ref[j, h] + cols[:, sb + j:sb + j + 1] * v8[j:j + 1, hs]
            s_out_ref[j, h] = sn
            blk_o[j:j + 1, hs] = jnp.sum(sn * cols[:, 2 * sb + j:2 * sb + j + 1], axis=0, keepdims=True)
    for j in range(sb):
        glu_j = glu8[j:j + 1, :]
        blk_y[j:j + 1, :] = (jnp.sum(cs_ref[j] * cw_ref[0:CONV_W - 1, :], axis=0, keepdims=True)
                             + glu_j * cw_ref[CONV_W - 1:CONV_W, :])
        cs_out_ref[j, 0:CONV_W - 2, :] = cs_ref[j, 1:CONV_W - 1, :]
        cs_out_ref[j, CONV_W - 2:CONV_W - 1, :] = glu_j
    oraw_s[rows, :] = blk_o[...]
    yc_s[rows, :] = blk_y[...]

    @pl.when(i == pl.num_programs(0) - 1)
    def _():
        oraw = oraw_s[...]
        oa = jnp.concatenate(
            [_rms(oraw[:, h * D_HEAD:(h + 1) * D_HEAD], hn_ref[:, h * D_HEAD:(h + 1) * D_HEAD])
             for h in range(N_HEADS)], axis=-1) * g_s[...]
        ob = _layernorm_silu(yc_s[...] + cb_ref[...], lng_ref[...], lnb_ref[...])
        o_ref[...] = (h_ref[...] + _dot(oa.astype(BF16), wout_ref[0:D_GROUP, :])
                      + _dot(ob.astype(BF16), wout_ref[D_GROUP:2 * D_GROUP, :]))


def _mix_sample_call(hs, layer, state_hgrn, state_conv, mixw, prev):
    n_s, d = hs.shape
    depth, _, d_in = mixw[1].shape
    sb = SAMPLE_BLOCK
    assert n_s % sb == 0 and 3 * sb <= D_HEAD
    whole = lambda i: (0, 0)
    s_spec = pl.BlockSpec((None, sb, N_HEADS, D_HEAD, D_HEAD), lambda i: (layer, i, 0, 0, 0))
    cs_spec = pl.BlockSpec((None, sb, CONV_W - 1, D_GROUP), lambda i: (layer, i, 0, 0))
    prev_specs, prev_args, aliases = _stacked_state_outputs(prev, 3 + len(mixw))
    scratch = [pltpu.VMEM((n_s, D_GROUP), F32) for _ in range(8)]
    scratch += [pltpu.VMEM((sb, D_GROUP), F32), pltpu.VMEM((sb, D_GROUP), F32)]
    return pl.pallas_call(
        functools.partial(_mix_sample_kernel, layer=layer, n_prev=len(prev_args)),
        grid=(n_s // sb,),
        in_specs=[pl.BlockSpec((n_s, d), whole)] + _mixer_weight_specs(layer, depth, d, d_in)
        + [s_spec, cs_spec] + prev_specs,
        out_specs=[pl.BlockSpec((n_s, d), whole), s_spec, cs_spec],
        out_shape=[jax.ShapeDtypeStruct(hs.shape, F32),
                   jax.ShapeDtypeStruct(state_hgrn.shape, F32),
                   jax.ShapeDtypeStruct(state_conv.shape, F32)],
        scratch_shapes=scratch,
        input_output_aliases=aliases,
        compiler_params=pltpu.CompilerParams(dimension_semantics=("arbitrary",), vmem_limit_bytes=VMEM_LIMIT),
        name=f"mix_sample_l{layer}",
    )(hs, *mixw, state_hgrn, state_conv, *prev_args)


def kernel(x_prompt, x_sample, state_hgrn, state_conv, p_prompt, p_sample, norm_ffn1, ffn1_w_gate, ffn1_w_up, ffn1_w_down, norm_mix, w_in, hgrn_lb, hgrn_norm, conv_w, conv_b, conv_ln_g, conv_ln_b, w_out, norm_ffn2, ffn2_w_gate, ffn2_w_up, ffn2_w_down, norm_ple, w_ple_gate, w_ple_in, norm_final):
    batch, seq, d = x_prompt.shape
    n_s = x_sample.shape[0] * x_sample.shape[1]
    depth = w_in.shape[0]
    assert w_in.shape[-1] == 6 * D_GROUP and hgrn_lb.shape == (depth, D_GROUP)
    assert state_hgrn.shape[2:] == (N_HEADS, D_HEAD, D_HEAD) and conv_w.shape[1:] == (CONV_W, D_GROUP)

    bf = lambda w: w.astype(BF16)
    row3 = lambda a: a.reshape(depth, 1, a.shape[-1])
    ffn1 = (row3(norm_ffn1), bf(ffn1_w_gate), bf(ffn1_w_up), bf(ffn1_w_down))
    ffn2 = (row3(norm_ffn2), bf(ffn2_w_gate), bf(ffn2_w_up), bf(ffn2_w_down))
    mixw = (row3(norm_mix), bf(w_in), hgrn_lb, row3(hgrn_norm), conv_w, row3(conv_b), row3(conv_ln_g),
            row3(conv_ln_b), bf(w_out))
    pp = p_prompt.reshape(depth, batch * seq, p_prompt.shape[-1])
    ps = p_sample.reshape(depth, n_s, p_sample.shape[-1])
    ple = (pp, ps, row3(norm_ple), bf(w_ple_gate), bf(w_ple_in))
    gain_final = norm_final.reshape(1, d)

    hp = x_prompt.reshape(batch * seq, d)
    hs = x_sample.reshape(n_s, d)
    new_p = new_s = None
    for layer in range(depth):
        hp, hs = _row_call(hp, hs, layer, *ffn1)
        hp, *new_p = _mix_prompt_call(hp, layer, batch, seq, mixw, new_p)
        hs, *new_s = _mix_sample_call(hs, layer, state_hgrn, state_conv, mixw, new_s)
        hp, hs = _row_call(hp, hs, layer, *ffn2, ple=ple, final_gain=gain_final if layer == depth - 1 else None)
    return (hp.reshape(x_prompt.shape), hs.reshape(x_sample.shape), new_p[0], new_p[1], new_s[0], new_s[1])
```

```python
import functools

import jax
import jax.numpy as jnp
from jax import lax
from jax.experimental import pallas as pl
from jax.experimental.pallas import tpu as pltpu

F32 = jnp.float32
BF16 = jnp.bfloat16

EPS = 1e-6
LN2 = 0.6931471805599453
N_HEADS = 4
D_HEAD = 128
MXU_WIDTH = 256
D_GROUP = 512
N_SLAB = D_GROUP // D_HEAD
CONV_W = 31
CHUNK = 64
SUB = 8
HIST = 32
PHASES = 4
CONV_BLOCK = PHASES * SUB

ROW_TILE = 512
MIX_TILE = 512
FF_SPLIT = 2
STAGE_ROWS_WIDE = 128
STAGE_ROWS_NARROW = 256
SAMPLE_BLOCK = 16
VMEM_LIMIT = 56 * 1024 * 1024


def _dot(a, b):
    return jnp.dot(a, b, preferred_element_type=F32)


def _dot_nt(a, b):
    return lax.dot_general(a, b, (((1,), (1,)), ((), ())), preferred_element_type=F32)


def _dot_tn(a, b):
    return lax.dot_general(a, b, (((0,), (0,)), ((), ())), preferred_element_type=F32)


def _rms(x, g):
    return x * lax.rsqrt(jnp.mean(x * x, axis=-1, keepdims=True) + EPS) * g


def _silu(x):
    return x * jax.nn.sigmoid(x)


def _lower_bound(lb_ref, layer):
    z = lb_ref[...]
    e = jnp.exp(z - jnp.max(z, axis=0, keepdims=True))
    sm = e / jnp.sum(e, axis=0, keepdims=True)
    c0 = sm[0:1, :]
    c = c0
    for i in range(1, layer + 1):
        c = c + sm[i:i + 1, :]
    return c - c0


def _gates(fr, lb):
    u = jnp.exp(-jnp.abs(fr))
    den = 1.0 / (1.0 + u)
    pos = fr >= 0.0
    f = jnp.where(pos, 1.0 + lb * u, u + lb) * den
    k = (1.0 - lb) * jnp.where(pos, u, 1.0) * den
    return f, k


def _layernorm_silu(y, g, b):
    mu = jnp.mean(y, axis=-1, keepdims=True)
    yc = y - mu
    return _silu(yc * lax.rsqrt(jnp.mean(yc * yc, axis=-1, keepdims=True) + EPS) * g + b)


def _load_cast(w_hbm, layer, dst, stage, sem):
    rows, width = stage.shape[1], dst.shape[1]
    n = dst.shape[0] // rows
    assert n * rows == dst.shape[0] and width <= stage.shape[2]

    def copy(c):
        return pltpu.make_async_copy(w_hbm.at[layer, pl.ds(c * rows, rows), :],
                                     stage.at[c % 2, :, pl.ds(0, width)], sem.at[c % 2])

    copy(0).start()
    for c in range(n):
        if c + 1 < n:
            copy(c + 1).start()
        copy(c).wait()
        dst[c * rows:(c + 1) * rows, :] = stage[c % 2, :, 0:width].astype(BF16)


def _row_kernel(*refs, layer, n_prompt_tiles, d_ff, do_ple, final_norm):
    it = iter(refs)
    xp_ref, xs_ref = next(it), next(it)
    pp_ref, ps_ref = (next(it), next(it)) if do_ple else (None, None)
    g_ref, wg_hbm, wu_hbm, wd_hbm = next(it), next(it), next(it), next(it)
    gp_ref, wpg_hbm, wpi_hbm = (next(it), next(it), next(it)) if do_ple else (None, None, None)
    gf_ref = next(it) if final_norm else None
    op_ref, os_ref = next(it), next(it)
    wg_ref, wu_ref, wd_ref = next(it), next(it), next(it)
    wpg_ref, wpi_ref = (next(it), next(it)) if do_ple else (None, None)
    stage_wide, stage_narrow, sem_wide, sem_narrow = next(it), next(it), next(it), next(it)

    i = pl.program_id(0)

    @pl.when(i == 0)
    def _():
        _load_cast(wg_hbm, layer, wg_ref, stage_wide, sem_wide)
        _load_cast(wu_hbm, layer, wu_ref, stage_wide, sem_wide)
        _load_cast(wd_hbm, layer, wd_ref, stage_narrow, sem_narrow)
        if do_ple:
            _load_cast(wpg_hbm, layer, wpg_ref, stage_narrow, sem_narrow)
            _load_cast(wpi_hbm, layer, wpi_ref, stage_narrow, sem_narrow)

    n_tiles = d_ff // MXU_WIDTH
    bounds = [((n_tiles * i) // FF_SPLIT) * MXU_WIDTH for i in range(FF_SPLIT)] + [d_ff]

    def process(x_ref, p_ref, o_ref):
        x = x_ref[...]
        xn = _rms(x, g_ref[...]).astype(BF16)
        acc = None
        for lo, hi in zip(bounds[:-1], bounds[1:]):
            gate = _dot(xn, wg_ref[:, lo:hi])
            up = _dot(xn, wu_ref[:, lo:hi])
            part = _dot((_silu(gate) * up).astype(BF16), wd_ref[lo:hi, :])
            acc = part if acc is None else acc + part
        h = x + 0.5 * acc
        if do_ple:
            gate = jax.nn.sigmoid(_dot(_rms(h, gp_ref[...]).astype(BF16), wpg_ref[...]))
            h = h + gate * _dot(p_ref[...].astype(BF16), wpi_ref[...])
        if final_norm:
            h = _rms(h, gf_ref[...])
        o_ref[...] = h

    @pl.when(i < n_prompt_tiles)
    def _():
        process(xp_ref, pp_ref, op_ref)

    @pl.when(i == n_prompt_tiles)
    def _():
        process(xs_ref, ps_ref, os_ref)


def _row_call(hp, hs, layer, g, wg, wu, wd, ple=None, final_gain=None):
    n_p, d = hp.shape
    n_s = hs.shape[0]
    d_ff = wg.shape[-1]
    assert n_p % ROW_TILE == 0 and d_ff % MXU_WIDTH == 0
    npt = n_p // ROW_TILE
    do_ple = ple is not None
    final_norm = final_gain is not None

    ptile = lambda i: (jnp.minimum(i, npt - 1), 0)
    whole = lambda i: (0, 0)
    lsel3 = lambda i: (layer, 0, 0)

    in_specs = [pl.BlockSpec((ROW_TILE, d), ptile), pl.BlockSpec((n_s, d), whole)]
    args = [hp, hs]
    if do_ple:
        pp, ps, gp, wpg, wpi = ple
        d_p = pp.shape[-1]
        in_specs += [pl.BlockSpec((None, ROW_TILE, d_p), lambda i: (layer, jnp.minimum(i, npt - 1), 0)),
                     pl.BlockSpec((None, n_s, d_p), lsel3)]
        args += [pp, ps]
    in_hbm = pl.BlockSpec(memory_space=pl.ANY)
    in_specs += [pl.BlockSpec((None, 1, d), lsel3), in_hbm, in_hbm, in_hbm]
    args += [g, wg, wu, wd]
    scratch = [pltpu.VMEM((d, d_ff), BF16), pltpu.VMEM((d, d_ff), BF16), pltpu.VMEM((d_ff, d), BF16)]
    if do_ple:
        in_specs += [pl.BlockSpec((None, 1, d), lsel3), in_hbm, in_hbm]
        args += [gp, wpg, wpi]
        scratch += [pltpu.VMEM((d, d), BF16), pltpu.VMEM((d_p, d), BF16)]
        assert d_p % STAGE_ROWS_NARROW == 0
    if final_norm:
        in_specs += [pl.BlockSpec((1, d), whole)]
        args += [final_gain]
    assert d % STAGE_ROWS_WIDE == 0 and d_ff % STAGE_ROWS_NARROW == 0 and d % STAGE_ROWS_NARROW == 0
    scratch += [pltpu.VMEM((2, STAGE_ROWS_WIDE, d_ff), F32), pltpu.VMEM((2, STAGE_ROWS_NARROW, d), F32),
                pltpu.SemaphoreType.DMA((2,)), pltpu.SemaphoreType.DMA((2,))]

    return pl.pallas_call(
        functools.partial(_row_kernel, layer=layer, n_prompt_tiles=npt, d_ff=d_ff, do_ple=do_ple,
                          final_norm=final_norm),
        grid=(npt + 1,),
        in_specs=in_specs,
        out_specs=[pl.BlockSpec((ROW_TILE, d), ptile), pl.BlockSpec((n_s, d), whole)],
        out_shape=[jax.ShapeDtypeStruct(hp.shape, F32), jax.ShapeDtypeStruct(hs.shape, F32)],
        scratch_shapes=scratch,
        compiler_params=pltpu.CompilerParams(dimension_semantics=("arbitrary",), vmem_limit_bytes=VMEM_LIMIT),
        name=f"rows_l{layer}_{'ple' if do_ple else 'ffn'}",
    )(*args)


N_LEVELS = CHUNK.bit_length() - 1


def _score_owner():
    t = lax.broadcasted_iota(jnp.int32, (CHUNK, CHUNK), 0)
    s = lax.broadcasted_iota(jnp.int32, (CHUNK, CHUNK), 1)
    return jnp.where(s > t, -1, jnp.where(s == t, N_LEVELS, 31 - lax.clz(t ^ s)))


def _hgrn_scores(q, k, lf, q_adj, owner):
    nb = CHUNK // SUB
    row = lax.broadcasted_iota(jnp.int32, (SUB, D_HEAD), 0)
    bcast = lambda xb, r: jnp.broadcast_to(xb[r:r + 1, :], (SUB, D_HEAD))

    g_blocks, ends = [], []
    carry = None
    for b in range(nb):
        xb = lf[SUB * b:SUB * (b + 1), :]
        for sft in (1, 2, 4):
            xb = xb + jnp.where(row >= sft, pltpu.roll(xb, sft, axis=0), 0.0)
        if carry is not None:
            xb = xb + carry
        carry = bcast(xb, SUB - 1)
        g_blocks.append(xb)
        ends.append(carry)
    g = jnp.concatenate(g_blocks, axis=0)

    def level_factor(level):
        bs = 1 << level
        if bs == 2:
            return [jnp.exp(-jnp.abs(gb - jnp.where(row < 4, bcast(gb, 1), bcast(gb, 5)))) for gb in g_blocks]
        if bs == 4:
            return [jnp.exp(-jnp.abs(gb - bcast(gb, 3))) for gb in g_blocks]
        n = bs // SUB
        out = []
        for b, gb in enumerate(g_blocks):
            ref = ends[(b // (2 * n)) * (2 * n) + n - 1]
            out.append(jnp.exp(gb - ref if (b // n) % 2 == 1 else ref - gb))
        return out

    a = jnp.where(owner == 0, _dot_nt(q_adj, k.astype(BF16)), 0.0)
    for level in range(1, N_LEVELS):
        e = jnp.concatenate(level_factor(level), axis=0)
        a = jnp.where(owner == level, _dot_nt((q * e).astype(BF16), (k * e).astype(BF16)), a)
    a = jnp.where(owner == N_LEVELS, jnp.sum(q * k, axis=-1, keepdims=True), a)

    g_last = jnp.concatenate([ends[nb - 1]] * nb, axis=0)
    q_dec = q * jnp.exp(g)
    k_dec = k * jnp.exp(g_last - g)
    return a.astype(BF16), q_dec.astype(BF16), k_dec.astype(BF16), jnp.exp(ends[nb - 1])


def _mix_prompt_kernel(h_ref, gm_ref, win_hbm, lb_ref, hn_ref, cw_ref, cb_ref, lng_ref, lnb_ref, wout_hbm,
                       *rest, layer, tt, n_prev):
    (o_ref, s_out_ref, c_out_ref,
     q_s, k_s, lf_s, qa_s, v_s, gs_s, oa_s, yc_s, gbuf, st_s, own_s, a_s, qd_s, kd_s, dec_s,
     win_ref, wout_ref, stage_in, stage_out, sem_in, sem_out) = rest[n_prev:]
    t = pl.program_id(1)

    @pl.when((pl.program_id(0) == 0) & (t == 0))
    def _():
        _load_cast(win_hbm, layer, win_ref, stage_in, sem_in)
        _load_cast(wout_hbm, layer, wout_ref, stage_out, sem_out)

    @pl.when(t == 0)
    def _():
        st_s[...] = jnp.zeros_like(st_s)
        gbuf[:, 0:HIST, :] = jnp.zeros((N_SLAB, HIST, D_HEAD), F32)

    own_s[...] = _score_owner()
    chunk_rows = lambda c: pl.ds(pl.multiple_of(c * CHUNK, CHUNK), CHUNK)

    xn = _rms(h_ref[...], gm_ref[...]).astype(BF16)
    group = lambda j: _dot(xn, win_ref[:, j * D_GROUP:(j + 1) * D_GROUP])
    q = _silu(group(0))
    fr = group(1)
    f, k = _gates(fr, _lower_bound(lb_ref, layer))
    odd = (lax.broadcasted_iota(jnp.int32, (tt, D_GROUP), 0) & 1) == 1
    q_s[...] = q
    qa_s[...] = (q * jnp.where(odd, f, 1.0)).astype(BF16)
    k_s[...] = k
    lf_s[...] = jnp.maximum(jnp.log(f), jnp.minimum(fr, 0.0) - LN2)
    v_s[...] = group(2).astype(BF16)
    gs_s[...] = _silu(group(3))
    glu = group(4) * jax.nn.sigmoid(group(5))
    for j in range(N_SLAB):
        gbuf[j, HIST:HIST + tt, :] = glu[:, j * D_HEAD:(j + 1) * D_HEAD]

    def conv_block(i):
        base = pl.multiple_of(i * CONV_BLOCK, CONV_BLOCK)
        for j in range(N_SLAB):
            js = slice(j * D_HEAD, (j + 1) * D_HEAD)
            accs = [jnp.broadcast_to(cb_ref[:, js], (SUB, D_HEAD)) for _ in range(PHASES)]
            for w in range(CONV_W):
                cw = cw_ref[w:w + 1, js]
                for p in range(PHASES):
                    start = base + (HIST - (CONV_W - 1) + p + w)
                    accs[p] = accs[p] + gbuf[j, pl.ds(start, SUB, stride=PHASES), :] * cw
            for p in range(PHASES):
                yc_s[j, pl.ds(base + p, SUB, stride=PHASES), :] = accs[p]

    def stash_scores(c):
        rows = chunk_rows(c)
        owner = own_s[...]
        for h in range(N_HEADS):
            hs = slice(h * D_HEAD, (h + 1) * D_HEAD)
            a_s[h], qd_s[h], kd_s[h], dec_s[h] = _hgrn_scores(q_s[rows, hs], k_s[rows, hs], lf_s[rows, hs],
                                                               qa_s[rows, hs], owner)

    def advance_state(c):
        rows = chunk_rows(c)
        for h in range(N_HEADS):
            hs = slice(h * D_HEAD, (h + 1) * D_HEAD)
            v = v_s[rows, hs]
            st = st_s[h]
            o = _dot_nt(qd_s[h], st.astype(BF16)) + _dot(a_s[h], v)
            st_s[h] = st * dec_s[h][0:1, :] + _dot_tn(v, kd_s[h])
            oa_s[rows, hs] = _rms(o, hn_ref[:, hs]) * gs_s[rows, hs]

    n_chunks = tt // CHUNK
    conv_per_chunk = tt // CONV_BLOCK // n_chunks

    def chunk_body(c, carry):
        advance_state(c - 1)
        stash_scores(c)
        for r in range(conv_per_chunk):
            conv_block(c * conv_per_chunk + r)
        return carry

    stash_scores(0)
    for r in range(conv_per_chunk):
        conv_block(r)
    lax.fori_loop(1, n_chunks, chunk_body, 0)
    advance_state(n_chunks - 1)

    ob = _layernorm_silu(jnp.concatenate([yc_s[j] for j in range(N_SLAB)], axis=-1), lng_ref[...], lnb_ref[...])
    o_ref[...] = (h_ref[...] + _dot(oa_s[...].astype(BF16), wout_ref[0:D_GROUP, :])
                  + _dot(ob.astype(BF16), wout_ref[D_GROUP:2 * D_GROUP, :]))

    @pl.when(t == pl.num_programs(1) - 1)
    def _():
        lo = tt + HIST - (CONV_W - 1)
        c_out_ref[...] = jnp.concatenate([gbuf[j, lo:lo + CONV_W - 1, :] for j in range(N_SLAB)], axis=-1)
        for h in range(N_HEADS):
            s_out_ref[h] = st_s[h].T

    gbuf[:, 0:HIST, :] = gbuf[:, tt:tt + HIST, :]


def _mixer_weight_specs(layer, depth, d, d_in):
    lsel3 = lambda *_: (layer, 0, 0)
    whole2 = lambda *_: (0, 0)
    in_hbm = pl.BlockSpec(memory_space=pl.ANY)
    return [pl.BlockSpec((None, 1, d), lsel3),
            in_hbm,
            pl.BlockSpec((depth, D_GROUP), whole2),
            pl.BlockSpec((None, 1, D_GROUP), lsel3),
            pl.BlockSpec((None, CONV_W, D_GROUP), lsel3),
            pl.BlockSpec((None, 1, D_GROUP), lsel3),
            pl.BlockSpec((None, 1, D_GROUP), lsel3),
            pl.BlockSpec((None, 1, D_GROUP), lsel3),
            in_hbm]


def _mixer_weight_scratch(d, d_in):
    assert d % STAGE_ROWS_WIDE == 0 and (2 * D_GROUP) % STAGE_ROWS_NARROW == 0
    return [pltpu.VMEM((d, d_in), BF16), pltpu.VMEM((2 * D_GROUP, d), BF16),
            pltpu.VMEM((2, STAGE_ROWS_WIDE, d_in), F32), pltpu.VMEM((2, STAGE_ROWS_NARROW, d), F32),
            pltpu.SemaphoreType.DMA((2,)), pltpu.SemaphoreType.DMA((2,))]


def _stacked_state_outputs(prev, n_args):
    if prev is None:
        return [], [], {}
    return ([pl.BlockSpec(memory_space=pl.ANY)] * len(prev), list(prev),
            {n_args + i: 1 + i for i in range(len(prev))})


def _mix_prompt_call(hp, layer, batch, seq, mixw, prev):
    n_p, d = hp.shape
    depth, _, d_in = mixw[1].shape
    tt = MIX_TILE
    assert seq % tt == 0 and tt % CHUNK == 0 and tt % CONV_BLOCK == 0 and tt >= HIST
    nt = seq // tt
    tile = lambda b, t: (b * nt + t, 0)
    prev_specs, prev_args, aliases = _stacked_state_outputs(prev, 1 + len(mixw))
    scratch = [pltpu.VMEM((tt, D_GROUP), F32),
               pltpu.VMEM((tt, D_GROUP), F32),
               pltpu.VMEM((tt, D_GROUP), F32),
               pltpu.VMEM((tt, D_GROUP), BF16),
               pltpu.VMEM((tt, D_GROUP), BF16),
               pltpu.VMEM((tt, D_GROUP), F32),
               pltpu.VMEM((tt, D_GROUP), F32),
               pltpu.VMEM((N_SLAB, tt, D_HEAD), F32),
               pltpu.VMEM((N_SLAB, tt + HIST, D_HEAD), F32),
               pltpu.VMEM((N_HEADS, D_HEAD, D_HEAD), F32),
               pltpu.VMEM((CHUNK, CHUNK), jnp.int32),
               pltpu.VMEM((N_HEADS, CHUNK, CHUNK), BF16),
               pltpu.VMEM((N_HEADS, CHUNK, D_HEAD), BF16),
               pltpu.VMEM((N_HEADS, CHUNK, D_HEAD), BF16),
               pltpu.VMEM((N_HEADS, SUB, D_HEAD), F32)]
    scratch += _mixer_weight_scratch(d, d_in)
    return pl.pallas_call(
        functools.partial(_mix_prompt_kernel, layer=layer, tt=tt, n_prev=len(prev_args)),
        grid=(batch, nt),
        in_specs=[pl.BlockSpec((tt, d), tile)] + _mixer_weight_specs(layer, depth, d, d_in) + prev_specs,
        out_specs=[pl.BlockSpec((tt, d), tile),
                   pl.BlockSpec((None, None, N_HEADS, D_HEAD, D_HEAD), lambda b, t: (layer, b, 0, 0, 0)),
                   pl.BlockSpec((None, None, CONV_W - 1, D_GROUP), lambda b, t: (layer, b, 0, 0))],
        out_shape=[jax.ShapeDtypeStruct(hp.shape, F32),
                   jax.ShapeDtypeStruct((depth, batch, N_HEADS, D_HEAD, D_HEAD), F32),
                   jax.ShapeDtypeStruct((depth, batch, CONV_W - 1, D_GROUP), F32)],
        scratch_shapes=scratch,
        input_output_aliases=aliases,
        compiler_params=pltpu.CompilerParams(dimension_semantics=("arbitrary", "arbitrary"),
                                             vmem_limit_bytes=VMEM_LIMIT),
        name=f"mix_prompt_l{layer}",
    )(hp, *mixw, *prev_args)


def _mix_sample_kernel(h_ref, gm_ref, win_hbm, lb_ref, hn_ref, cw_ref, cb_ref, lng_ref, lnb_ref, wout_hbm,
                       s_ref, cs_ref, *rest, layer, n_prev):
    (o_ref, s_out_ref, cs_out_ref,
     q_s, f_s, k_s, v_s, g_s, glu_s, oraw_s, yc_s, blk_o, blk_y,
     win_ref, wout_ref, stage_in, stage_out, sem_in, sem_out) = rest[n_prev:]
    i = pl.program_id(0)
    sb = SAMPLE_BLOCK

    @pl.when(i == 0)
    def _():
        _load_cast(win_hbm, layer, win_ref, stage_in, sem_in)
        _load_cast(wout_hbm, layer, wout_ref, stage_out, sem_out)
        proj =_dot(_rms(h_ref[...], gm_ref[...]).astype(BF16), win_ref[...])
        part = lambda j: proj[:, j * D_GROUP:(j + 1) * D_GROUP]
        f, k = _gates(part(1), _lower_bound(lb_ref, layer))
        q_s[...] = _silu(part(0))
        f_s[...] = f
        k_s[...] = k
        v_s[...] = part(2)
        g_s[...] = _silu(part(3))
        glu_s[...] = part(4) * jax.nn.sigmoid(part(5))

    r0 = pl.multiple_of(i * sb, sb)
    rows = pl.ds(r0, sb)
    f8, k8, q8, v8, glu8 = f_s[rows, :], k_s[rows, :], q_s[rows, :], v_s[rows, :], glu_s[rows, :]
    pad = jnp.zeros((D_HEAD - 3 * sb, D_HEAD), F32)
    for h in range(N_HEADS):
        hs = slice(h * D_HEAD, (h + 1) * D_HEAD)
        cols = jnp.concatenate([f8[:, hs], k8[:, hs], q8[:, hs], pad], axis=0).T
        for j in range(sb):
            sn = cols[:, j:j + 1] * s_ref[j, h] + cols[:, sb + j:sb + j + 1] * v8[j:j + 1, hs]
            s_out_ref[j, h] = sn
            blk_o[j:j + 1, hs] = jnp.sum(sn * cols[:, 2 * sb + j:2 * sb + j + 1], axis=0, keepdims=True)
    for j in range(sb):
        glu_j = glu8[j:j + 1, :]
        blk_y[j:j + 1, :] = (jnp.sum(cs_ref[j] * cw_ref[0:CONV_W - 1, :], axis=0, keepdims=True)
                             + glu_j * cw_ref[CONV_W - 1:CONV_W, :])
        cs_out_ref[j, 0:CONV_W - 2, :] = cs_ref[j, 1:CONV_W - 1, :]
        cs_out_ref[j, CONV_W - 2:CONV_W - 1, :] = glu_j
    oraw_s[rows, :] = blk_o[...]
    yc_s[rows, :] = blk_y[...]

    @pl.when(i == pl.num_programs(0) - 1)
    def _():
        oraw = oraw_s[...]
        oa = jnp.concatenate(
            [_rms(oraw[:, h * D_HEAD:(h + 1) * D_HEAD], hn_ref[:, h * D_HEAD:(h + 1) * D_HEAD])
             for h in range(N_HEADS)], axis=-1) * g_s[...]
        ob = _layernorm_silu(yc_s[...] + cb_ref[...], lng_ref[...], lnb_ref[...])
        o_ref[...] = (h_ref[...] + _dot(oa.astype(BF16), wout_ref[0:D_GROUP, :])
                      + _dot(ob.astype(BF16), wout_ref[D_GROUP:2 * D_GROUP, :]))


def _mix_sample_call(hs, layer, state_hgrn, state_conv, mixw, prev):
    n_s, d = hs.shape
    depth, _, d_in = mixw[1].shape
    sb = SAMPLE_BLOCK
    assert n_s % sb == 0 and 3 * sb <= D_HEAD
    whole = lambda i: (0, 0)
    s_spec = pl.BlockSpec((None, sb, N_HEADS, D_HEAD, D_HEAD), lambda i: (layer, i, 0, 0, 0))
    cs_spec = pl.BlockSpec((None, sb, CONV_W - 1, D_GROUP), lambda i: (layer, i, 0, 0))
    prev_specs, prev_args, aliases = _stacked_state_outputs(prev, 3 + len(mixw))
    scratch = [pltpu.VMEM((n_s, D_GROUP), F32) for _ in range(8)]
    scratch += [pltpu.VMEM((sb, D_GROUP), F32), pltpu.VMEM((sb, D_GROUP), F32)]
    scratch += _mixer_weight_scratch(d, d_in)
    return pl.pallas_call(
        functools.partial(_mix_sample_kernel, layer=layer, n_prev=len(prev_args)),
        grid=(n_s // sb,),
        in_specs=[pl.BlockSpec((n_s, d), whole)] + _mixer_weight_specs(layer, depth, d, d_in)
        + [s_spec, cs_spec] + prev_specs,
        out_specs=[pl.BlockSpec((n_s, d), whole), s_spec, cs_spec],
        out_shape=[jax.ShapeDtypeStruct(hs.shape, F32),
                   jax.ShapeDtypeStruct(state_hgrn.shape, F32),
                   jax.ShapeDtypeStruct(state_conv.shape, F32)],
        scratch_shapes=scratch,
        input_output_aliases=aliases,
        compiler_params=pltpu.CompilerParams(dimension_semantics=("arbitrary",), vmem_limit_bytes=VMEM_LIMIT),
        name=f"mix_sample_l{layer}",
    )(hs, *mixw, state_hgrn, state_conv, *prev_args)


def kernel(x_prompt, x_sample, state_hgrn, state_conv, p_prompt, p_sample, norm_ffn1, ffn1_w_gate, ffn1_w_up, ffn1_w_down, norm_mix, w_in, hgrn_lb, hgrn_norm, conv_w, conv_b, conv_ln_g, conv_ln_b, w_out, norm_ffn2, ffn2_w_gate, ffn2_w_up, ffn2_w_down, norm_ple, w_ple_gate, w_ple_in, norm_final):
    batch, seq, d = x_prompt.shape
    n_s = x_sample.shape[0] * x_sample.shape[1]
    depth = w_in.shape[0]
    assert w_in.shape[-1] == 6 * D_GROUP and hgrn_lb.shape == (depth, D_GROUP)
    assert state_hgrn.shape[2:] == (N_HEADS, D_HEAD, D_HEAD) and conv_w.shape[1:] == (CONV_W, D_GROUP)

    row3 = lambda a: a.reshape(depth, 1, a.shape[-1])
    ffn1 = (row3(norm_ffn1), ffn1_w_gate, ffn1_w_up, ffn1_w_down)
    ffn2 = (row3(norm_ffn2), ffn2_w_gate, ffn2_w_up, ffn2_w_down)
    mixw = (row3(norm_mix), w_in, hgrn_lb, row3(hgrn_norm), conv_w, row3(conv_b), row3(conv_ln_g),
            row3(conv_ln_b), w_out)
    pp = p_prompt.reshape(depth, batch * seq, p_prompt.shape[-1])
    ps = p_sample.reshape(depth, n_s, p_sample.shape[-1])
    ple = (pp, ps, row3(norm_ple), w_ple_gate, w_ple_in)
    gain_final = norm_final.reshape(1, d)

    hp = x_prompt.reshape(batch * seq, d)
    hs = x_sample.reshape(n_s, d)
    new_p = new_s = None
    for layer in range(depth):
        hp, hs = _row_call(hp, hs, layer, *ffn1)
        hp, *new_p = _mix_prompt_call(hp, layer, batch, seq, mixw, new_p)
        hs, *new_s = _mix_sample_call(hs, layer, state_hgrn, state_conv, mixw, new_s)
        hp, hs = _row_call(hp, hs, layer, *ffn2, ple=ple, final_gain=gain_final if layer == depth - 1 else None)
    return (hp.reshape(x_prompt.shape), hs.reshape(x_sample.shape), new_p[0], new_p[1], new_s[0], new_s[1])
```

```python
import functools

import jax
import jax.numpy as jnp
from jax import lax
from jax.experimental import pallas as pl
from jax.experimental.pallas import tpu as pltpu

F32 = jnp.float32
BF16 = jnp.bfloat16

EPS = 1e-6
LN2 = 0.6931471805599453
N_HEADS = 4
D_HEAD = 128
MXU_WIDTH = 256
D_GROUP = 512
N_SLAB = D_GROUP // D_HEAD
CONV_W = 31
CHUNK = 64
SUB = 8
HIST = 32
PHASES = 4
CONV_BLOCK = PHASES * SUB

ROW_TILE = 512
MIX_TILE = 512
FF_SPLIT = 2
STAGE_ROWS = 128
STAGE_SLOTS = 4
SAMPLE_BLOCK = 16
VMEM_LIMIT = 56 * 1024 * 1024


def _dot(a, b):
    return jnp.dot(a, b, preferred_element_type=F32)


def _dot_nt(a, b):
    return lax.dot_general(a, b, (((1,), (1,)), ((), ())), preferred_element_type=F32)


def _dot_tn(a, b):
    return lax.dot_general(a, b, (((0,), (0,)), ((), ())), preferred_element_type=F32)


def _rms(x, g):
    return x * lax.rsqrt(jnp.mean(x * x, axis=-1, keepdims=True) + EPS) * g


def _silu(x):
    return x * jax.nn.sigmoid(x)


def _lower_bound(lb_ref, layer):
    z = lb_ref[...]
    e = jnp.exp(z - jnp.max(z, axis=0, keepdims=True))
    sm = e / jnp.sum(e, axis=0, keepdims=True)
    c0 = sm[0:1, :]
    c = c0
    for i in range(1, layer + 1):
        c = c + sm[i:i + 1, :]
    return c - c0


def _gates(fr, lb):
    u = jnp.exp(-jnp.abs(fr))
    den = 1.0 / (1.0 + u)
    pos = fr >= 0.0
    f = jnp.where(pos, 1.0 + lb * u, u + lb) * den
    k = (1.0 - lb) * jnp.where(pos, u, 1.0) * den
    return f, k


def _layernorm_silu(y, g, b):
    mu = jnp.mean(y, axis=-1, keepdims=True)
    yc = y - mu
    return _silu(yc * lax.rsqrt(jnp.mean(yc * yc, axis=-1, keepdims=True) + EPS) * g + b)


def _load_cast(jobs, layer, stage, sem):
    slots, rows = stage.shape[0], stage.shape[1]
    chunks = []
    for w_hbm, dst in jobs:
        assert dst.shape[0] % rows == 0 and dst.shape[1] <= stage.shape[2]
        chunks += [(w_hbm, dst, c * rows) for c in range(dst.shape[0] // rows)]

    def copy(i):
        w_hbm, dst, r0 = chunks[i]
        return pltpu.make_async_copy(w_hbm.at[layer, pl.ds(r0, rows), :],
                                     stage.at[i % slots, :, pl.ds(0, dst.shape[1])], sem.at[i % slots])

    for i in range(min(slots - 1, len(chunks))):
        copy(i).start()
    for i, (_, dst, r0) in enumerate(chunks):
        if i + slots - 1 < len(chunks):
            copy(i + slots - 1).start()
        copy(i).wait()
        dst[r0:r0 + rows, :] = stage[i % slots, :, 0:dst.shape[1]].astype(BF16)


def _weight_stage(width):
    return [pltpu.VMEM((STAGE_SLOTS, STAGE_ROWS, width), F32), pltpu.SemaphoreType.DMA((STAGE_SLOTS,))]


def _row_kernel(*refs, layer, n_prompt_tiles, d_ff, do_ple, final_norm):
    it = iter(refs)
    xp_ref, xs_ref = next(it), next(it)
    pp_ref, ps_ref = (next(it), next(it)) if do_ple else (None, None)
    g_ref, wg_hbm, wu_hbm, wd_hbm = next(it), next(it), next(it), next(it)
    gp_ref, wpg_hbm, wpi_hbm = (next(it), next(it), next(it)) if do_ple else (None, None, None)
    gf_ref = next(it) if final_norm else None
    op_ref, os_ref = next(it), next(it)
    wg_ref, wu_ref, wd_ref = next(it), next(it), next(it)
    wpg_ref, wpi_ref = (next(it), next(it)) if do_ple else (None, None)
    stage, sem = next(it), next(it)

    i = pl.program_id(0)

    @pl.when(i == 0)
    def _():
        jobs = [(wg_hbm, wg_ref), (wu_hbm, wu_ref), (wd_hbm, wd_ref)]
        if do_ple:
            jobs += [(wpg_hbm, wpg_ref), (wpi_hbm, wpi_ref)]
        _load_cast(jobs, layer, stage, sem)

    n_tiles = d_ff // MXU_WIDTH
    bounds = [((n_tiles * i) // FF_SPLIT) * MXU_WIDTH for i in range(FF_SPLIT)] + [d_ff]

    def process(x_ref, p_ref, o_ref):
        x = x_ref[...]
        xn = _rms(x, g_ref[...]).astype(BF16)
        acc = None
        for lo, hi in zip(bounds[:-1], bounds[1:]):
            gate = _dot(xn, wg_ref[:, lo:hi])
            up = _dot(xn, wu_ref[:, lo:hi])
            part = _dot((_silu(gate) * up).astype(BF16), wd_ref[lo:hi, :])
            acc = part if acc is None else acc + part
        h = x + 0.5 * acc
        if do_ple:
            gate = jax.nn.sigmoid(_dot(_rms(h, gp_ref[...]).astype(BF16), wpg_ref[...]))
            h = h + gate * _dot(p_ref[...].astype(BF16), wpi_ref[...])
        if final_norm:
            h = _rms(h, gf_ref[...])
        o_ref[...] = h

    @pl.when(i < n_prompt_tiles)
    def _():
        process(xp_ref, pp_ref, op_ref)

    @pl.when(i == n_prompt_tiles)
    def _():
        process(xs_ref, ps_ref, os_ref)


def _row_call(hp, hs, layer, g, wg, wu, wd, ple=None, final_gain=None):
    n_p, d = hp.shape
    n_s = hs.shape[0]
    d_ff = wg.shape[-1]
    assert n_p % ROW_TILE == 0 and d_ff % MXU_WIDTH == 0
    npt = n_p // ROW_TILE
    do_ple = ple is not None
    final_norm = final_gain is not None

    ptile = lambda i: (jnp.minimum(i, npt - 1), 0)
    whole = lambda i: (0, 0)
    lsel3 = lambda i: (layer, 0, 0)

    in_specs = [pl.BlockSpec((ROW_TILE, d), ptile), pl.BlockSpec((n_s, d), whole)]
    args = [hp, hs]
    if do_ple:
        pp, ps, gp, wpg, wpi = ple
        d_p = pp.shape[-1]
        in_specs += [pl.BlockSpec((None, ROW_TILE, d_p), lambda i: (layer, jnp.minimum(i, npt - 1), 0)),
                     pl.BlockSpec((None, n_s, d_p), lsel3)]
        args += [pp, ps]
    in_hbm = pl.BlockSpec(memory_space=pl.ANY)
    in_specs += [pl.BlockSpec((None, 1, d), lsel3), in_hbm, in_hbm, in_hbm]
    args += [g, wg, wu, wd]
    scratch = [pltpu.VMEM((d, d_ff), BF16), pltpu.VMEM((d, d_ff), BF16), pltpu.VMEM((d_ff, d), BF16)]
    if do_ple:
        in_specs += [pl.BlockSpec((None, 1, d), lsel3), in_hbm, in_hbm]
        args += [gp, wpg, wpi]
        scratch += [pltpu.VMEM((d, d), BF16), pltpu.VMEM((d_p, d), BF16)]
    if final_norm:
        in_specs += [pl.BlockSpec((1, d), whole)]
        args += [final_gain]
    scratch += _weight_stage(max(d, d_ff))

    return pl.pallas_call(
        functools.partial(_row_kernel, layer=layer, n_prompt_tiles=npt, d_ff=d_ff, do_ple=do_ple,
                          final_norm=final_norm),
        grid=(npt + 1,),
        in_specs=in_specs,
        out_specs=[pl.BlockSpec((ROW_TILE, d), ptile), pl.BlockSpec((n_s, d), whole)],
        out_shape=[jax.ShapeDtypeStruct(hp.shape, F32), jax.ShapeDtypeStruct(hs.shape, F32)],
        scratch_shapes=scratch,
        compiler_params=pltpu.CompilerParams(dimension_semantics=("arbitrary",), vmem_limit_bytes=VMEM_LIMIT),
        name=f"rows_l{layer}_{'ple' if do_ple else 'ffn'}",
    )(*args)


N_LEVELS = CHUNK.bit_length() - 1


def _score_owner():
    t = lax.broadcasted_iota(jnp.int32, (CHUNK, CHUNK), 0)
    s = lax.broadcasted_iota(jnp.int32, (CHUNK, CHUNK), 1)
    return jnp.where(s > t, -1, jnp.where(s == t, N_LEVELS, 31 - lax.clz(t ^ s)))


def _hgrn_scores(q, k, lf, q_adj, owner):
    nb = CHUNK // SUB
    row = lax.broadcasted_iota(jnp.int32, (SUB, D_HEAD), 0)
    bcast = lambda xb, r: jnp.broadcast_to(xb[r:r + 1, :], (SUB, D_HEAD))

    g_blocks, ends = [], []
    carry = None
    for b in range(nb):
        xb = lf[SUB * b:SUB * (b + 1), :]
        for sft in (1, 2, 4):
            xb = xb + jnp.where(row >= sft, pltpu.roll(xb, sft, axis=0), 0.0)
        if carry is not None:
            xb = xb + carry
        carry = bcast(xb, SUB - 1)
        g_blocks.append(xb)
        ends.append(carry)
    g = jnp.concatenate(g_blocks, axis=0)

    def level_factor(level):
        bs = 1 << level
        if bs == 2:
            return [jnp.exp(-jnp.abs(gb - jnp.where(row < 4, bcast(gb, 1), bcast(gb, 5)))) for gb in g_blocks]
        if bs == 4:
            return [jnp.exp(-jnp.abs(gb - bcast(gb, 3))) for gb in g_blocks]
        n = bs // SUB
        out = []
        for b, gb in enumerate(g_blocks):
            ref = ends[(b // (2 * n)) * (2 * n) + n - 1]
            out.append(jnp.exp(gb - ref if (b // n) % 2 == 1 else ref - gb))
        return out

    a = jnp.where(owner == 0, _dot_nt(q_adj, k.astype(BF16)), 0.0)
    for level in range(1, N_LEVELS):
        e = jnp.concatenate(level_factor(level), axis=0)
        a = jnp.where(owner == level, _dot_nt((q * e).astype(BF16), (k * e).astype(BF16)), a)
    a = jnp.where(owner == N_LEVELS, jnp.sum(q * k, axis=-1, keepdims=True), a)

    g_last = jnp.concatenate([ends[nb - 1]] * nb, axis=0)
    q_dec = q * jnp.exp(g)
    k_dec = k * jnp.exp(g_last - g)
    return a.astype(BF16), q_dec.astype(BF16), k_dec.astype(BF16), jnp.exp(ends[nb - 1])


def _mix_prompt_kernel(h_ref, gm_ref, win_hbm, lb_ref, hn_ref, cw_ref, cb_ref, lng_ref, lnb_ref, wout_hbm,
                       *rest, layer, tt, n_prev):
    (o_ref, s_out_ref, c_out_ref,
     q_s, k_s, lf_s, qa_s, v_s, gs_s, oa_s, yc_s, gbuf, st_s, own_s, a_s, qd_s, kd_s, dec_s,
     win_ref, wout_ref, stage, sem) = rest[n_prev:]
    t = pl.program_id(1)

    @pl.when((pl.program_id(0) == 0) & (t == 0))
    def _():
        _load_cast([(win_hbm, win_ref), (wout_hbm, wout_ref)], layer, stage, sem)

    @pl.when(t == 0)
    def _():
        st_s[...] = jnp.zeros_like(st_s)
        gbuf[:, 0:HIST, :] = jnp.zeros((N_SLAB, HIST, D_HEAD), F32)

    own_s[...] = _score_owner()
    chunk_rows = lambda c: pl.ds(pl.multiple_of(c * CHUNK, CHUNK), CHUNK)

    xn = _rms(h_ref[...], gm_ref[...]).astype(BF16)
    group = lambda j: _dot(xn, win_ref[:, j * D_GROUP:(j + 1) * D_GROUP])
    q = _silu(group(0))
    fr = group(1)
    f, k = _gates(fr, _lower_bound(lb_ref, layer))
    odd = (lax.broadcasted_iota(jnp.int32, (tt, D_GROUP), 0) & 1) == 1
    q_s[...] = q
    qa_s[...] = (q * jnp.where(odd, f, 1.0)).astype(BF16)
    k_s[...] = k
    lf_s[...] = jnp.maximum(jnp.log(f), jnp.minimum(fr, 0.0) - LN2)
    v_s[...] = group(2).astype(BF16)
    gs_s[...] = _silu(group(3))
    glu = group(4) * jax.nn.sigmoid(group(5))
    for j in range(N_SLAB):
        gbuf[j, HIST:HIST + tt, :] = glu[:, j * D_HEAD:(j + 1) * D_HEAD]

    def conv_block(i):
        base = pl.multiple_of(i * CONV_BLOCK, CONV_BLOCK)
        for j in range(N_SLAB):
            js = slice(j * D_HEAD, (j + 1) * D_HEAD)
            accs = [jnp.broadcast_to(cb_ref[:, js], (SUB, D_HEAD)) for _ in range(PHASES)]
            for w in range(CONV_W):
                cw = cw_ref[w:w + 1, js]
                for p in range(PHASES):
                    start = base + (HIST - (CONV_W - 1) + p + w)
                    accs[p] = accs[p] + gbuf[j, pl.ds(start, SUB, stride=PHASES), :] * cw
            for p in range(PHASES):
                yc_s[j, pl.ds(base + p, SUB, stride=PHASES), :] = accs[p]

    def stash_scores(c):
        rows = chunk_rows(c)
        owner = own_s[...]
        for h in range(N_HEADS):
            hs = slice(h * D_HEAD, (h + 1) * D_HEAD)
            a_s[h], qd_s[h], kd_s[h], dec_s[h] = _hgrn_scores(q_s[rows, hs], k_s[rows, hs], lf_s[rows, hs],
                                                               qa_s[rows, hs], owner)

    def advance_state(c):
        rows = chunk_rows(c)
        for h in range(N_HEADS):
            hs = slice(h * D_HEAD, (h + 1) * D_HEAD)
            v = v_s[rows, hs]
            st = st_s[h]
            o = _dot_nt(qd_s[h], st.astype(BF16)) + _dot(a_s[h], v)
            st_s[h] = st * dec_s[h][0:1, :] + _dot_tn(v, kd_s[h])
            oa_s[rows, hs] = _rms(o, hn_ref[:, hs]) * gs_s[rows, hs]

    n_chunks = tt // CHUNK
    conv_per_chunk = tt // CONV_BLOCK // n_chunks

    def chunk_body(c, carry):
        advance_state(c - 1)
        stash_scores(c)
        for r in range(conv_per_chunk):
            conv_block(c * conv_per_chunk + r)
        return carry

    stash_scores(0)
    for r in range(conv_per_chunk):
        conv_block(r)
    lax.fori_loop(1, n_chunks, chunk_body, 0)
    advance_state(n_chunks - 1)

    ob = _layernorm_silu(jnp.concatenate([yc_s[j] for j in range(N_SLAB)], axis=-1), lng_ref[...], lnb_ref[...])
    o_ref[...] = (h_ref[...] + _dot(oa_s[...].astype(BF16), wout_ref[0:D_GROUP, :])
                  + _dot(ob.astype(BF16), wout_ref[D_GROUP:2 * D_GROUP, :]))

    @pl.when(t == pl.num_programs(1) - 1)
    def _():
        lo = tt + HIST - (CONV_W - 1)
        c_out_ref[...] = jnp.concatenate([gbuf[j, lo:lo + CONV_W - 1, :] for j in range(N_SLAB)], axis=-1)
        for h in range(N_HEADS):
            s_out_ref[h] = st_s[h].T

    gbuf[:, 0:HIST, :] = gbuf[:, tt:tt + HIST, :]


def _mixer_weight_specs(layer, depth, d, d_in):
    lsel3 = lambda *_: (layer, 0, 0)
    whole2 = lambda *_: (0, 0)
    in_hbm = pl.BlockSpec(memory_space=pl.ANY)
    return [pl.BlockSpec((None, 1, d), lsel3),
            in_hbm,
            pl.BlockSpec((depth, D_GROUP), whole2),
            pl.BlockSpec((None, 1, D_GROUP), lsel3),
            pl.BlockSpec((None, CONV_W, D_GROUP), lsel3),
            pl.BlockSpec((None, 1, D_GROUP), lsel3),
            pl.BlockSpec((None, 1, D_GROUP), lsel3),
            pl.BlockSpec((None, 1, D_GROUP), lsel3),
            in_hbm]


def _mixer_weight_scratch(d, d_in):
    return [pltpu.VMEM((d, d_in), BF16), pltpu.VMEM((2 * D_GROUP, d), BF16)] + _weight_stage(max(d, d_in))


def _stacked_state_outputs(prev, n_args):
    if prev is None:
        return [], [], {}
    return ([pl.BlockSpec(memory_space=pl.ANY)] * len(prev), list(prev),
            {n_args + i: 1 + i for i in range(len(prev))})


def _mix_prompt_call(hp, layer, batch, seq, mixw, prev):
    n_p, d = hp.shape
    depth, _, d_in = mixw[1].shape
    tt = MIX_TILE
    assert seq % tt == 0 and tt % CHUNK == 0 and tt % CONV_BLOCK == 0 and tt >= HIST
    nt = seq // tt
    tile = lambda b, t: (b * nt + t, 0)
    prev_specs, prev_args, aliases = _stacked_state_outputs(prev, 1 + len(mixw))
    scratch = [pltpu.VMEM((tt, D_GROUP), F32),
               pltpu.VMEM((tt, D_GROUP), F32),
               pltpu.VMEM((tt, D_GROUP), F32),
               pltpu.VMEM((tt, D_GROUP), BF16),
               pltpu.VMEM((tt, D_GROUP), BF16),
               pltpu.VMEM((tt, D_GROUP), F32),
               pltpu.VMEM((tt, D_GROUP), F32),
               pltpu.VMEM((N_SLAB, tt, D_HEAD), F32),
               pltpu.VMEM((N_SLAB, tt + HIST, D_HEAD), F32),
               pltpu.VMEM((N_HEADS, D_HEAD, D_HEAD), F32),
               pltpu.VMEM((CHUNK, CHUNK), jnp.int32),
               pltpu.VMEM((N_HEADS, CHUNK, CHUNK), BF16),
               pltpu.VMEM((N_HEADS, CHUNK, D_HEAD), BF16),
               pltpu.VMEM((N_HEADS, CHUNK, D_HEAD), BF16),
               pltpu.VMEM((N_HEADS, SUB, D_HEAD), F32)]
    scratch += _mixer_weight_scratch(d, d_in)
    return pl.pallas_call(
        functools.partial(_mix_prompt_kernel, layer=layer, tt=tt, n_prev=len(prev_args)),
        grid=(batch, nt),
        in_specs=[pl.BlockSpec((tt, d), tile)] + _mixer_weight_specs(layer, depth, d, d_in) + prev_specs,
        out_specs=[pl.BlockSpec((tt, d), tile),
                   pl.BlockSpec((None, None, N_HEADS, D_HEAD, D_HEAD), lambda b, t: (layer, b, 0, 0, 0)),
                   pl.BlockSpec((None, None, CONV_W - 1, D_GROUP), lambda b, t: (layer, b, 0, 0))],
        out_shape=[jax.ShapeDtypeStruct(hp.shape, F32),
                   jax.ShapeDtypeStruct((depth, batch, N_HEADS, D_HEAD, D_HEAD), F32),
                   jax.ShapeDtypeStruct((depth, batch, CONV_W - 1, D_GROUP), F32)],
        scratch_shapes=scratch,
        input_output_aliases=aliases,
        compiler_params=pltpu.CompilerParams(dimension_semantics=("arbitrary", "arbitrary"),
                                             vmem_limit_bytes=VMEM_LIMIT),
        name=f"mix_prompt_l{layer}",
    )(hp, *mixw, *prev_args)


def _mix_sample_kernel(h_ref, gm_ref, win_hbm, lb_ref, hn_ref, cw_ref, cb_ref, lng_ref, lnb_ref, wout_hbm,
                       s_ref, cs_ref, *rest, layer, n_prev):
    (o_ref, s_out_ref, cs_out_ref,
     q_s, f_s, k_s, v_s, g_s, glu_s, oraw_s, yc_s, blk_o, blk_y,
     win_ref, wout_ref, stage, sem) = rest[n_prev:]
    i = pl.program_id(0)
    sb = SAMPLE_BLOCK

    @pl.when(i == 0)
    def _():
        _load_cast([(win_hbm, win_ref), (wout_hbm, wout_ref)], layer, stage, sem)
        proj =_dot(_rms(h_ref[...], gm_ref[...]).astype(BF16), win_ref[...])
        part = lambda j: proj[:, j * D_GROUP:(j + 1) * D_GROUP]
        f, k = _gates(part(1), _lower_bound(lb_ref, layer))
        q_s[...] = _silu(part(0))
        f_s[...] = f
        k_s[...] = k
        v_s[...] = part(2)
        g_s[...] = _silu(part(3))
        glu_s[...] = part(4) * jax.nn.sigmoid(part(5))

    r0 = pl.multiple_of(i * sb, sb)
    rows = pl.ds(r0, sb)
    f8, k8, q8, v8, glu8 = f_s[rows, :], k_s[rows, :], q_s[rows, :], v_s[rows, :], glu_s[rows, :]
    pad = jnp.zeros((D_HEAD - 3 * sb, D_HEAD), F32)
    for h in range(N_HEADS):
        hs = slice(h * D_HEAD, (h + 1) * D_HEAD)
        cols = jnp.concatenate([f8[:, hs], k8[:, hs], q8[:, hs], pad], axis=0).T
        for j in range(sb):
            sn = cols[:, j:j + 1] * s_ref[j, h] + cols[:, sb + j:sb + j + 1] * v8[j:j + 1, hs]
            s_out_ref[j, h] = sn
            blk_o[j:j + 1, hs] = jnp.sum(sn * cols[:, 2 * sb + j:2 * sb + j + 1], axis=0, keepdims=True)
    for j in range(sb):
        glu_j = glu8[j:j + 1, :]
        blk_y[j:j + 1, :] = (jnp.sum(cs_ref[j] * cw_ref[0:CONV_W - 1, :], axis=0, keepdims=True)
                             + glu_j * cw_ref[CONV_W - 1:CONV_W, :])
        cs_out_ref[j, 0:CONV_W - 2, :] = cs_ref[j, 1:CONV_W - 1, :]
        cs_out_ref[j, CONV_W - 2:CONV_W - 1, :] = glu_j
    oraw_s[rows, :] = blk_o[...]
    yc_s[rows, :] = blk_y[...]

    @pl.when(i == pl.num_programs(0) - 1)
    def _():
        oraw = oraw_s[...]
        oa = jnp.concatenate(
            [_rms(oraw[:, h * D_HEAD:(h + 1) * D_HEAD], hn_ref[:, h * D_HEAD:(h + 1) * D_HEAD])
             for h in range(N_HEADS)], axis=-1) * g_s[...]
        ob = _layernorm_silu(yc_s[...] + cb_ref[...], lng_ref[...], lnb_ref[...])
        o_ref[...] = (h_ref[...] + _dot(oa.astype(BF16), wout_ref[0:D_GROUP, :])
                      + _dot(ob.astype(BF16), wout_ref[D_GROUP:2 * D_GROUP, :]))


def _mix_sample_call(hs, layer, state_hgrn, state_conv, mixw, prev):
    n_s, d = hs.shape
    depth, _, d_in = mixw[1].shape
    sb = SAMPLE_BLOCK
    assert n_s % sb == 0 and 3 * sb <= D_HEAD
    whole = lambda i: (0, 0)
    s_spec = pl.BlockSpec((None, sb, N_HEADS, D_HEAD, D_HEAD), lambda i: (layer, i, 0, 0, 0))
    cs_spec = pl.BlockSpec((None, sb, CONV_W - 1, D_GROUP), lambda i: (layer, i, 0, 0))
    prev_specs, prev_args, aliases = _stacked_state_outputs(prev, 3 + len(mixw))
    scratch = [pltpu.VMEM((n_s, D_GROUP), F32) for _ in range(8)]
    scratch += [pltpu.VMEM((sb, D_GROUP), F32), pltpu.VMEM((sb, D_GROUP), F32)]
    scratch += _mixer_weight_scratch(d, d_in)
    return pl.pallas_call(
        functools.partial(_mix_sample_kernel, layer=layer, n_prev=len(prev_args)),
        grid=(n_s // sb,),
        in_specs=[pl.BlockSpec((n_s, d), whole)] + _mixer_weight_specs(layer, depth, d, d_in)
        + [s_spec, cs_spec] + prev_specs,
        out_specs=[pl.BlockSpec((n_s, d), whole), s_spec, cs_spec],
        out_shape=[jax.ShapeDtypeStruct(hs.shape, F32),
                   jax.ShapeDtypeStruct(state_hgrn.shape, F32),
                   jax.ShapeDtypeStruct(state_conv.shape, F32)],
        scratch_shapes=scratch,
        input_output_aliases=aliases,
        compiler_params=pltpu.CompilerParams(dimension_semantics=("arbitrary",), vmem_limit_bytes=VMEM_LIMIT),
        name=f"mix_sample_l{layer}",
    )(hs, *mixw, state_hgrn, state_conv, *prev_args)


def kernel(x_prompt, x_sample, state_hgrn, state_conv, p_prompt, p_sample, norm_ffn1, ffn1_w_gate, ffn1_w_up, ffn1_w_down, norm_mix, w_in, hgrn_lb, hgrn_norm, conv_w, conv_b, conv_ln_g, conv_ln_b, w_out, norm_ffn2, ffn2_w_gate, ffn2_w_up, ffn2_w_down, norm_ple, w_ple_gate, w_ple_in, norm_final):
    batch, seq, d = x_prompt.shape
    n_s = x_sample.shape[0] * x_sample.shape[1]
    depth = w_in.shape[0]
    assert w_in.shape[-1] == 6 * D_GROUP and hgrn_lb.shape == (depth, D_GROUP)
    assert state_hgrn.shape[2:] == (N_HEADS, D_HEAD, D_HEAD) and conv_w.shape[1:] == (CONV_W, D_GROUP)

    row3 = lambda a: a.reshape(depth, 1, a.shape[-1])
    ffn1 = (row3(norm_ffn1), ffn1_w_gate, ffn1_w_up, ffn1_w_down)
    ffn2 = (row3(norm_ffn2), ffn2_w_gate, ffn2_w_up, ffn2_w_down)
    mixw = (row3(norm_mix), w_in, hgrn_lb, row3(hgrn_norm), conv_w, row3(conv_b), row3(conv_ln_g),
            row3(conv_ln_b), w_out)
    pp = p_prompt.reshape(depth, batch * seq, p_prompt.shape[-1])
    ps = p_sample.reshape(depth, n_s, p_sample.shape[-1])
    ple = (pp, ps, row3(norm_ple), w_ple_gate, w_ple_in)
    gain_final = norm_final.reshape(1, d)

    hp = x_prompt.reshape(batch * seq, d)
    hs = x_sample.reshape(n_s, d)
    new_p = new_s = None
    for layer in range(depth):
        hp, hs = _row_call(hp, hs, layer, *ffn1)
        hp, *new_p = _mix_prompt_call(hp, layer, batch, seq, mixw, new_p)
        hs, *new_s = _mix_sample_call(hs, layer, state_hgrn, state_conv, mixw, new_s)
        hp, hs = _row_call(hp, hs, layer, *ffn2, ple=ple, final_gain=gain_final if layer == depth - 1 else None)
    return (hp.reshape(x_prompt.shape), hs.reshape(x_sample.shape), new_p[0], new_p[1], new_s[0], new_s[1])
```

```python
import functools

import jax
import jax.numpy as jnp
from jax import lax
from jax.experimental import pallas as pl
from jax.experimental.pallas import tpu as pltpu

F32 = jnp.float32
BF16 = jnp.bfloat16

EPS = 1e-6
LN2 = 0.6931471805599453
N_HEADS = 4
D_HEAD = 128
MXU_WIDTH = 256
D_GROUP = 512
N_SLAB = D_GROUP // D_HEAD
CONV_W = 31
CHUNK = 64
SUB = 8
HIST = 32
PHASES = 4
CONV_BLOCK = PHASES * SUB

ROW_TILE = 512
MIX_TILE = 512
FF_SPLIT = 2
STAGE_ROWS = 128
STAGE_SLOTS = 4
SAMPLE_BLOCK = 16
VMEM_LIMIT = 56 * 1024 * 1024


def _dot(a, b):
    return jnp.dot(a, b, preferred_element_type=F32)


def _dot_nt(a, b):
    return lax.dot_general(a, b, (((1,), (1,)), ((), ())), preferred_element_type=F32)


def _dot_tn(a, b):
    return lax.dot_general(a, b, (((0,), (0,)), ((), ())), preferred_element_type=F32)


def _rms(x, g):
    return x * lax.rsqrt(jnp.mean(x * x, axis=-1, keepdims=True) + EPS) * g


def _silu(x):
    return x * jax.nn.sigmoid(x)


def _lower_bound(lb_ref, layer):
    z = lb_ref[...]
    e = jnp.exp(z - jnp.max(z, axis=0, keepdims=True))
    sm = e / jnp.sum(e, axis=0, keepdims=True)
    c0 = sm[0:1, :]
    c = c0
    for i in range(1, layer + 1):
        c = c + sm[i:i + 1, :]
    return c - c0


def _gates(fr, lb):
    u = jnp.exp(-jnp.abs(fr))
    den = 1.0 / (1.0 + u)
    pos = fr >= 0.0
    f = jnp.where(pos, 1.0 + lb * u, u + lb) * den
    k = (1.0 - lb) * jnp.where(pos, u, 1.0) * den
    return f, k


def _layernorm_silu(y, g, b):
    mu = jnp.mean(y, axis=-1, keepdims=True)
    yc = y - mu
    return _silu(yc * lax.rsqrt(jnp.mean(yc * yc, axis=-1, keepdims=True) + EPS) * g + b)


def _load_cast(jobs, layer, stage, sem):
    slots, rows = stage.shape[0], stage.shape[1]
    chunks = []
    for w_hbm, dst in jobs:
        assert dst.shape[0] % rows == 0 and dst.shape[1] <= stage.shape[2]
        chunks += [(w_hbm, dst, c * rows) for c in range(dst.shape[0] // rows)]

    def copy(i):
        w_hbm, dst, r0 = chunks[i]
        return pltpu.make_async_copy(w_hbm.at[layer, pl.ds(r0, rows), :],
                                     stage.at[i % slots, :, pl.ds(0, dst.shape[1])], sem.at[i % slots])

    for i in range(min(slots - 1, len(chunks))):
        copy(i).start()
    for i, (_, dst, r0) in enumerate(chunks):
        if i + slots - 1 < len(chunks):
            copy(i + slots - 1).start()
        copy(i).wait()
        dst[r0:r0 + rows, :] = stage[i % slots, :, 0:dst.shape[1]].astype(BF16)


def _weight_stage(width):
    return [pltpu.VMEM((STAGE_SLOTS, STAGE_ROWS, width), F32), pltpu.SemaphoreType.DMA((STAGE_SLOTS,))]


def _row_kernel(*refs, layer, n_prompt_tiles, d_ff, do_ple, final_norm):
    it = iter(refs)
    xp_ref, xs_ref = next(it), next(it)
    pp_ref, ps_ref = (next(it), next(it)) if do_ple else (None, None)
    g_ref, wg_hbm, wu_hbm, wd_hbm = next(it), next(it), next(it), next(it)
    gp_ref, wpg_hbm, wpi_hbm = (next(it), next(it), next(it)) if do_ple else (None, None, None)
    gf_ref = next(it) if final_norm else None
    op_ref, os_ref = next(it), next(it)
    wg_ref, wu_ref, wd_ref = next(it), next(it), next(it)
    wpg_ref, wpi_ref = (next(it), next(it)) if do_ple else (None, None)
    stage, sem = next(it), next(it)

    i = pl.program_id(0)

    @pl.when(i == 0)
    def _():
        jobs = [(wg_hbm, wg_ref), (wu_hbm, wu_ref), (wd_hbm, wd_ref)]
        if do_ple:
            jobs += [(wpg_hbm, wpg_ref), (wpi_hbm, wpi_ref)]
        _load_cast(jobs, layer, stage, sem)

    n_tiles = d_ff // MXU_WIDTH
    bounds = [((n_tiles * i) // FF_SPLIT) * MXU_WIDTH for i in range(FF_SPLIT)] + [d_ff]

    def process(x_ref, p_ref, o_ref):
        x = x_ref[...]
        xn = _rms(x, g_ref[...]).astype(BF16)
        acc = None
        for lo, hi in zip(bounds[:-1], bounds[1:]):
            gate = _dot(xn, wg_ref[:, lo:hi])
            up = _dot(xn, wu_ref[:, lo:hi])
            part = _dot((_silu(gate) * up).astype(BF16), wd_ref[lo:hi, :])
            acc = part if acc is None else acc + part
        h = x + 0.5 * acc
        if do_ple:
            gate = jax.nn.sigmoid(_dot(_rms(h, gp_ref[...]).astype(BF16), wpg_ref[...]))
            h = h + gate * _dot(p_ref[...].astype(BF16), wpi_ref[...])
        if final_norm:
            h = _rms(h, gf_ref[...])
        o_ref[...] = h

    @pl.when(i < n_prompt_tiles)
    def _():
        process(xp_ref, pp_ref, op_ref)

    @pl.when(i == n_prompt_tiles)
    def _():
        process(xs_ref, ps_ref, os_ref)


def _row_call(hp, hs, layer, g, wg, wu, wd, ple=None, final_gain=None):
    n_p, d = hp.shape
    n_s = hs.shape[0]
    d_ff = wg.shape[-1]
    assert n_p % ROW_TILE == 0 and d_ff % MXU_WIDTH == 0
    npt = n_p // ROW_TILE
    do_ple = ple is not None
    final_norm = final_gain is not None

    ptile = lambda i: (jnp.minimum(i, npt - 1), 0)
    whole = lambda i: (0, 0)
    lsel3 = lambda i: (layer, 0, 0)

    in_specs = [pl.BlockSpec((ROW_TILE, d), ptile), pl.BlockSpec((n_s, d), whole)]
    args = [hp, hs]
    if do_ple:
        pp, ps, gp, wpg, wpi = ple
        d_p = pp.shape[-1]
        in_specs += [pl.BlockSpec((None, ROW_TILE, d_p), lambda i: (layer, jnp.minimum(i, npt - 1), 0)),
                     pl.BlockSpec((None, n_s, d_p), lsel3)]
        args += [pp, ps]
    in_hbm = pl.BlockSpec(memory_space=pl.ANY)
    in_specs += [pl.BlockSpec((None, 1, d), lsel3), in_hbm, in_hbm, in_hbm]
    args += [g, wg, wu, wd]
    scratch = [pltpu.VMEM((d, d_ff), BF16), pltpu.VMEM((d, d_ff), BF16), pltpu.VMEM((d_ff, d), BF16)]
    if do_ple:
        in_specs += [pl.BlockSpec((None, 1, d), lsel3), in_hbm, in_hbm]
        args += [gp, wpg, wpi]
        scratch += [pltpu.VMEM((d, d), BF16), pltpu.VMEM((d_p, d), BF16)]
    if final_norm:
        in_specs += [pl.BlockSpec((1, d), whole)]
        args += [final_gain]
    scratch += _weight_stage(max(d, d_ff))

    return pl.pallas_call(
        functools.partial(_row_kernel, layer=layer, n_prompt_tiles=npt, d_ff=d_ff, do_ple=do_ple,
                          final_norm=final_norm),
        grid=(npt + 1,),
        in_specs=in_specs,
        out_specs=[pl.BlockSpec((ROW_TILE, d), ptile), pl.BlockSpec((n_s, d), whole)],
        out_shape=[jax.ShapeDtypeStruct(hp.shape, F32), jax.ShapeDtypeStruct(hs.shape, F32)],
        scratch_shapes=scratch,
        compiler_params=pltpu.CompilerParams(dimension_semantics=("arbitrary",), vmem_limit_bytes=VMEM_LIMIT),
        name=f"rows_l{layer}_{'ple' if do_ple else 'ffn'}",
    )(*args)


N_LEVELS = CHUNK.bit_length() - 1


def _score_owner():
    t = lax.broadcasted_iota(jnp.int32, (CHUNK, CHUNK), 0)
    s = lax.broadcasted_iota(jnp.int32, (CHUNK, CHUNK), 1)
    return jnp.where(s > t, -1, jnp.where(s == t, N_LEVELS, 31 - lax.clz(t ^ s)))


def _hgrn_scores(q, k, lf, q_adj, owner):
    nb = CHUNK // SUB
    row = lax.broadcasted_iota(jnp.int32, (SUB, D_HEAD), 0)
    bcast = lambda xb, r: jnp.broadcast_to(xb[r:r + 1, :], (SUB, D_HEAD))

    g_blocks, ends = [], []
    carry = None
    for b in range(nb):
        xb = lf[SUB * b:SUB * (b + 1), :]
        for sft in (1, 2, 4):
            xb = xb + jnp.where(row >= sft, pltpu.roll(xb, sft, axis=0), 0.0)
        if carry is not None:
            xb = xb + carry
        carry = bcast(xb, SUB - 1)
        g_blocks.append(xb)
        ends.append(carry)
    g = jnp.concatenate(g_blocks, axis=0)

    def level_factor(level):
        bs = 1 << level
        if bs == 2:
            return [jnp.exp(-jnp.abs(gb - jnp.where(row < 4, bcast(gb, 1), bcast(gb, 5)))) for gb in g_blocks]
        if bs == 4:
            return [jnp.exp(-jnp.abs(gb - bcast(gb, 3))) for gb in g_blocks]
        n = bs // SUB
        out = []
        for b, gb in enumerate(g_blocks):
            ref = ends[(b // (2 * n)) * (2 * n) + n - 1]
            out.append(jnp.exp(gb - ref if (b // n) % 2 == 1 else ref - gb))
        return out

    a = jnp.where(owner == 0, _dot_nt(q_adj, k.astype(BF16)), 0.0)
    for level in range(1, N_LEVELS):
        e = jnp.concatenate(level_factor(level), axis=0)
        a = jnp.where(owner == level, _dot_nt((q * e).astype(BF16), (k * e).astype(BF16)), a)
    a = jnp.where(owner == N_LEVELS, jnp.sum(q * k, axis=-1, keepdims=True), a)

    g_last = jnp.concatenate([ends[nb - 1]] * nb, axis=0)
    q_dec = q * jnp.exp(g)
    k_dec = k * jnp.exp(g_last - g)
    return a.astype(BF16), q_dec.astype(BF16), k_dec.astype(BF16), jnp.exp(ends[nb - 1])


def _mix_prompt_kernel(h_ref, gm_ref, win_hbm, lb_ref, hn_ref, cw_ref, cb_ref, lng_ref, lnb_ref, wout_hbm,
                       *rest, layer, tt, n_prev):
    (o_ref, s_out_ref, c_out_ref,
     q_s, k_s, lf_s, qa_s, v_s, gs_s, oa_s, yc_s, gbuf, st_s, own_s, a_s, qd_s, kd_s, dec_s,
     win_ref, wout_ref, stage, sem) = rest[n_prev:]
    t = pl.program_id(1)

    @pl.when((pl.program_id(0) == 0) & (t == 0))
    def _():
        _load_cast([(win_hbm, win_ref), (wout_hbm, wout_ref)], layer, stage, sem)

    @pl.when(t == 0)
    def _():
        st_s[...] = jnp.zeros_like(st_s)
        gbuf[:, 0:HIST, :] = jnp.zeros((N_SLAB, HIST, D_HEAD), F32)

    own_s[...] = _score_owner()
    chunk_rows = lambda c: pl.ds(pl.multiple_of(c * CHUNK, CHUNK), CHUNK)

    xn = _rms(h_ref[...], gm_ref[...]).astype(BF16)
    group = lambda j: _dot(xn, win_ref[:, j * D_GROUP:(j + 1) * D_GROUP])
    q = _silu(group(0))
    fr = group(1)
    f, k = _gates(fr, _lower_bound(lb_ref, layer))
    odd = (lax.broadcasted_iota(jnp.int32, (tt, D_GROUP), 0) & 1) == 1
    q_s[...] = q
    qa_s[...] = (q * jnp.where(odd, f, 1.0)).astype(BF16)
    k_s[...] = k
    lf_s[...] = jnp.maximum(jnp.log(f), jnp.minimum(fr, 0.0) - LN2)
    v_s[...] = group(2).astype(BF16)
    gs_s[...] = _silu(group(3))
    glu = group(4) * jax.nn.sigmoid(group(5))
    for j in range(N_SLAB):
        gbuf[j, HIST:HIST + tt, :] = glu[:, j * D_HEAD:(j + 1) * D_HEAD]

    def conv_block(i):
        base = pl.multiple_of(i * CONV_BLOCK, CONV_BLOCK)
        for j in range(N_SLAB):
            js = slice(j * D_HEAD, (j + 1) * D_HEAD)
            accs = [jnp.broadcast_to(cb_ref[:, js], (SUB, D_HEAD)) for _ in range(PHASES)]
            for w in range(CONV_W):
                cw = cw_ref[w:w + 1, js]
                for p in range(PHASES):
                    start = base + (HIST - (CONV_W - 1) + p + w)
                    accs[p] = accs[p] + gbuf[j, pl.ds(start, SUB, stride=PHASES), :] * cw
            for p in range(PHASES):
                yc_s[j, pl.ds(base + p, SUB, stride=PHASES), :] = accs[p]

    def stash_scores(c):
        rows = chunk_rows(c)
        owner = own_s[...]
        for h in range(N_HEADS):
            hs = slice(h * D_HEAD, (h + 1) * D_HEAD)
            a_s[h], qd_s[h], kd_s[h], dec_s[h] = _hgrn_scores(q_s[rows, hs], k_s[rows, hs], lf_s[rows, hs],
                                                               qa_s[rows, hs], owner)

    def advance_state(c):
        rows = chunk_rows(c)
        for h in range(N_HEADS):
            hs = slice(h * D_HEAD, (h + 1) * D_HEAD)
            v = v_s[rows, hs]
            st = st_s[h]
            o = _dot_nt(qd_s[h], st.astype(BF16)) + _dot(a_s[h], v)
            st_s[h] = st * dec_s[h][0:1, :] + _dot_tn(v, kd_s[h])
            oa_s[rows, hs] = _rms(o, hn_ref[:, hs]) * gs_s[rows, hs]

    n_chunks = tt // CHUNK
    conv_per_chunk = tt // CONV_BLOCK // n_chunks

    def chunk_body(c, carry):
        advance_state(c - 1)
        stash_scores(c)
        for r in range(conv_per_chunk):
            conv_block(c * conv_per_chunk + r)
        return carry

    stash_scores(0)
    for r in range(conv_per_chunk):
        conv_block(r)
    lax.fori_loop(1, n_chunks, chunk_body, 0)
    advance_state(n_chunks - 1)

    ob = _layernorm_silu(jnp.concatenate([yc_s[j] for j in range(N_SLAB)], axis=-1), lng_ref[...], lnb_ref[...])
    o_ref[...] = (h_ref[...] + _dot(oa_s[...].astype(BF16), wout_ref[0:D_GROUP, :])
                  + _dot(ob.astype(BF16), wout_ref[D_GROUP:2 * D_GROUP, :]))

    @pl.when(t == pl.num_programs(1) - 1)
    def _():
        lo = tt + HIST - (CONV_W - 1)
        c_out_ref[...] = jnp.concatenate([gbuf[j, lo:lo + CONV_W - 1, :] for j in range(N_SLAB)], axis=-1)
        for h in range(N_HEADS):
            s_out_ref[h] = st_s[h].T

    gbuf[:, 0:HIST, :] = gbuf[:, tt:tt + HIST, :]


def _mixer_weight_specs(layer, depth, d, d_in):
    lsel3 = lambda *_: (layer, 0, 0)
    whole2 = lambda *_: (0, 0)
    in_hbm = pl.BlockSpec(memory_space=pl.ANY)
    return [pl.BlockSpec((None, 1, d), lsel3),
            in_hbm,
            pl.BlockSpec((depth, D_GROUP), whole2),
            pl.BlockSpec((None, 1, D_GROUP), lsel3),
            pl.BlockSpec((None, CONV_W, D_GROUP), lsel3),
            pl.BlockSpec((None, 1, D_GROUP), lsel3),
            pl.BlockSpec((None, 1, D_GROUP), lsel3),
            pl.BlockSpec((None, 1, D_GROUP), lsel3),
            in_hbm]


def _mixer_weight_scratch(d, d_in):
    return [pltpu.VMEM((d, d_in), BF16), pltpu.VMEM((2 * D_GROUP, d), BF16)] + _weight_stage(max(d, d_in))


def _stacked_state_outputs(prev, n_args):
    if prev is None:
        return [], [], {}
    return ([pl.BlockSpec(memory_space=pl.ANY)] * len(prev), list(prev),
            {n_args + i: 1 + i for i in range(len(prev))})


def _mix_prompt_call(hp, layer, batch, seq, mixw, prev):
    n_p, d = hp.shape
    depth, _, d_in = mixw[1].shape
    tt = MIX_TILE
    assert seq % tt == 0 and tt % CHUNK == 0 and tt % CONV_BLOCK == 0 and tt >= HIST
    nt = seq // tt
    tile = lambda b, t: (b * nt + t, 0)
    prev_specs, prev_args, aliases = _stacked_state_outputs(prev, 1 + len(mixw))
    scratch = [pltpu.VMEM((tt, D_GROUP), F32),
               pltpu.VMEM((tt, D_GROUP), F32),
               pltpu.VMEM((tt, D_GROUP), F32),
               pltpu.VMEM((tt, D_GROUP), BF16),
               pltpu.VMEM((tt, D_GROUP), BF16),
               pltpu.VMEM((tt, D_GROUP), F32),
               pltpu.VMEM((tt, D_GROUP), F32),
               pltpu.VMEM((N_SLAB, tt, D_HEAD), F32),
               pltpu.VMEM((N_SLAB, tt + HIST, D_HEAD), F32),
               pltpu.VMEM((N_HEADS, D_HEAD, D_HEAD), F32),
               pltpu.VMEM((CHUNK, CHUNK), jnp.int32),
               pltpu.VMEM((N_HEADS, CHUNK, CHUNK), BF16),
               pltpu.VMEM((N_HEADS, CHUNK, D_HEAD), BF16),
               pltpu.VMEM((N_HEADS, CHUNK, D_HEAD), BF16),
               pltpu.VMEM((N_HEADS, SUB, D_HEAD), F32)]
    scratch += _mixer_weight_scratch(d, d_in)
    return pl.pallas_call(
        functools.partial(_mix_prompt_kernel, layer=layer, tt=tt, n_prev=len(prev_args)),
        grid=(batch, nt),
        in_specs=[pl.BlockSpec((tt, d), tile)] + _mixer_weight_specs(layer, depth, d, d_in) + prev_specs,
        out_specs=[pl.BlockSpec((tt, d), tile),
                   pl.BlockSpec((None, None, N_HEADS, D_HEAD, D_HEAD), lambda b, t: (layer, b, 0, 0, 0)),
                   pl.BlockSpec((None, None, CONV_W - 1, D_GROUP), lambda b, t: (layer, b, 0, 0))],
        out_shape=[jax.ShapeDtypeStruct(hp.shape, F32),
                   jax.ShapeDtypeStruct((depth, batch, N_HEADS, D_HEAD, D_HEAD), F32),
                   jax.ShapeDtypeStruct((depth, batch, CONV_W - 1, D_GROUP), F32)],
        scratch_shapes=scratch,
        input_output_aliases=aliases,
        compiler_params=pltpu.CompilerParams(dimension_semantics=("arbitrary", "arbitrary"),
                                             vmem_limit_bytes=VMEM_LIMIT),
        name=f"mix_prompt_l{layer}",
    )(hp, *mixw, *prev_args)


def _mix_sample_kernel(h_ref, gm_ref, win_hbm, lb_ref, hn_ref, cw_ref, cb_ref, lng_ref, lnb_ref, wout_hbm,
                       s_ref, cs_ref, *rest, layer, n_prev):
    (o_ref, s_out_ref, cs_out_ref,
     q_s, f_s, v_s, g_s, glu_s, oraw_s, yc_s, blk_o,
     win_ref, wout_ref, stage, sem) = rest[n_prev:]
    i = pl.program_id(0)
    sb = SAMPLE_BLOCK

    @pl.when(i == 0)
    def _():
        _load_cast([(win_hbm, win_ref), (wout_hbm, wout_ref)], layer, stage, sem)
        proj =_dot(_rms(h_ref[...], gm_ref[...]).astype(BF16), win_ref[...])
        part = lambda j: proj[:, j * D_GROUP:(j + 1) * D_GROUP]
        f, _ = _gates(part(1), _lower_bound(lb_ref, layer))
        q_s[...] = _silu(part(0))
        f_s[...] = f
        v_s[...] = part(2)
        g_s[...] = _silu(part(3))
        glu_s[...] = part(4) * jax.nn.sigmoid(part(5))

    r0 = pl.multiple_of(i * sb, sb)
    rows = pl.ds(r0, sb)
    f8, q8, v8, glu8 = f_s[rows, :], q_s[rows, :], v_s[rows, :], glu_s[rows, :]
    pad = jnp.zeros((D_HEAD - 2 * sb, D_HEAD), F32)
    for h in range(N_HEADS):
        hs = slice(h * D_HEAD, (h + 1) * D_HEAD)
        cols = jnp.concatenate([f8[:, hs], q8[:, hs], pad], axis=0).T
        for j in range(sb):
            vj = v8[j:j + 1, hs]
            sn = vj + cols[:, j:j + 1] * (s_ref[j, h] - vj)
            s_out_ref[j, h] = sn
            blk_o[j:j + 1, hs] = jnp.sum(sn * cols[:, sb + j:sb + j + 1], axis=0, keepdims=True)
    yc = glu8 * cw_ref[CONV_W - 1:CONV_W, :]
    for w in range(CONV_W - 1):
        yc = yc + cs_ref[w] * cw_ref[w:w + 1, :]
    for w in range(CONV_W - 2):
        cs_out_ref[w] = cs_ref[w + 1]
    cs_out_ref[CONV_W - 2] = glu8
    oraw_s[rows, :] = blk_o[...]
    yc_s[rows, :] = yc

    @pl.when(i == pl.num_programs(0) - 1)
    def _():
        oraw = oraw_s[...]
        oa = jnp.concatenate(
            [_rms(oraw[:, h * D_HEAD:(h + 1) * D_HEAD], hn_ref[:, h * D_HEAD:(h + 1) * D_HEAD])
             for h in range(N_HEADS)], axis=-1) * g_s[...]
        ob = _layernorm_silu(yc_s[...] + cb_ref[...], lng_ref[...], lnb_ref[...])
        o_ref[...] = (h_ref[...] + _dot(oa.astype(BF16), wout_ref[0:D_GROUP, :])
                      + _dot(ob.astype(BF16), wout_ref[D_GROUP:2 * D_GROUP, :]))


def _mix_sample_call(hs, layer, state_hgrn, state_conv, mixw, prev):
    n_s, d = hs.shape
    depth, _, d_in = mixw[1].shape
    sb = SAMPLE_BLOCK
    assert n_s % sb == 0 and 2 * sb <= D_HEAD
    whole = lambda i: (0, 0)
    s_spec = pl.BlockSpec((None, sb, N_HEADS, D_HEAD, D_HEAD), lambda i: (layer, i, 0, 0, 0))
    cs_spec = pl.BlockSpec((None, CONV_W - 1, sb, D_GROUP), lambda i: (layer, 0, i, 0))
    prev_specs, prev_args, aliases = _stacked_state_outputs(prev, 3 + len(mixw))
    scratch = [pltpu.VMEM((n_s, D_GROUP), F32) for _ in range(7)]
    scratch += [pltpu.VMEM((sb, D_GROUP), F32)]
    scratch += _mixer_weight_scratch(d, d_in)
    return pl.pallas_call(
        functools.partial(_mix_sample_kernel, layer=layer, n_prev=len(prev_args)),
        grid=(n_s // sb,),
        in_specs=[pl.BlockSpec((n_s, d), whole)] + _mixer_weight_specs(layer, depth, d, d_in)
        + [s_spec, cs_spec] + prev_specs,
        out_specs=[pl.BlockSpec((n_s, d), whole), s_spec, cs_spec],
        out_shape=[jax.ShapeDtypeStruct(hs.shape, F32),
                   jax.ShapeDtypeStruct(state_hgrn.shape, F32),
                   jax.ShapeDtypeStruct(state_conv.shape, F32)],
        scratch_shapes=scratch,
        input_output_aliases=aliases,
        compiler_params=pltpu.CompilerParams(dimension_semantics=("arbitrary",), vmem_limit_bytes=VMEM_LIMIT),
        name=f"mix_sample_l{layer}",
    )(hs, *mixw, state_hgrn, state_conv, *prev_args)


def kernel(x_prompt, x_sample, state_hgrn, state_conv, p_prompt, p_sample, norm_ffn1, ffn1_w_gate, ffn1_w_up, ffn1_w_down, norm_mix, w_in, hgrn_lb, hgrn_norm, conv_w, conv_b, conv_ln_g, conv_ln_b, w_out, norm_ffn2, ffn2_w_gate, ffn2_w_up, ffn2_w_down, norm_ple, w_ple_gate, w_ple_in, norm_final):
    batch, seq, d = x_prompt.shape
    n_s = x_sample.shape[0] * x_sample.shape[1]
    depth = w_in.shape[0]
    assert w_in.shape[-1] == 6 * D_GROUP and hgrn_lb.shape == (depth, D_GROUP)
    assert state_hgrn.shape[2:] == (N_HEADS, D_HEAD, D_HEAD) and conv_w.shape[1:] == (CONV_W, D_GROUP)

    row3 = lambda a: a.reshape(depth, 1, a.shape[-1])
    ffn1 = (row3(norm_ffn1), ffn1_w_gate, ffn1_w_up, ffn1_w_down)
    ffn2 = (row3(norm_ffn2), ffn2_w_gate, ffn2_w_up, ffn2_w_down)
    mixw = (row3(norm_mix), w_in, hgrn_lb, row3(hgrn_norm), conv_w, row3(conv_b), row3(conv_ln_g),
            row3(conv_ln_b), w_out)
    pp = p_prompt.reshape(depth, batch * seq, p_prompt.shape[-1])
    ps = p_sample.reshape(depth, n_s, p_sample.shape[-1])
    ple = (pp, ps, row3(norm_ple), w_ple_gate, w_ple_in)
    gain_final = norm_final.reshape(1, d)
    conv_taps = jnp.swapaxes(state_conv, 1, 2)

    hp = x_prompt.reshape(batch * seq, d)
    hs = x_sample.reshape(n_s, d)
    new_p = new_s = None
    for layer in range(depth):
        hp, hs = _row_call(hp, hs, layer, *ffn1)
        hp, *new_p = _mix_prompt_call(hp, layer, batch, seq, mixw, new_p)
        hs, *new_s = _mix_sample_call(hs, layer, state_hgrn, conv_taps, mixw, new_s)
        hp, hs = _row_call(hp, hs, layer, *ffn2, ple=ple, final_gain=gain_final if layer == depth - 1 else None)
    return (hp.reshape(x_prompt.shape), hs.reshape(x_sample.shape), new_p[0], new_p[1], new_s[0],
            jnp.swapaxes(new_s[1], 1, 2))
```

```python
import functools

import jax
import jax.numpy as jnp
from jax import lax
from jax.experimental import pallas as pl
from jax.experimental.pallas import tpu as pltpu

F32 = jnp.float32
BF16 = jnp.bfloat16

EPS = 1e-6
LN2 = 0.6931471805599453
N_HEADS = 4
D_HEAD = 128
MXU_WIDTH = 256
D_GROUP = 512
N_SLAB = D_GROUP // D_HEAD
CONV_W = 31
CHUNK = 64
SUB = 8
HIST = 32
PHASES = 4
CONV_BLOCK = PHASES * SUB

ROW_TILE = 512
MIX_TILE = 512
FF_SPLIT = 2
STAGE_ROWS = 256
STAGE_SLOTS = 3
SAMPLE_BLOCK = 16
VMEM_LIMIT = 56 * 1024 * 1024


def _dot(a, b):
    return jnp.dot(a, b, preferred_element_type=F32)


def _dot_nt(a, b):
    return lax.dot_general(a, b, (((1,), (1,)), ((), ())), preferred_element_type=F32)


def _dot_tn(a, b):
    return lax.dot_general(a, b, (((0,), (0,)), ((), ())), preferred_element_type=F32)


def _rms(x, g):
    return x * lax.rsqrt(jnp.mean(x * x, axis=-1, keepdims=True) + EPS) * g


def _silu(x):
    return x * jax.nn.sigmoid(x)


def _lower_bound(lb_ref, layer):
    z = lb_ref[...]
    e = jnp.exp(z - jnp.max(z, axis=0, keepdims=True))
    sm = e / jnp.sum(e, axis=0, keepdims=True)
    c0 = sm[0:1, :]
    c = c0
    for i in range(1, layer + 1):
        c = c + sm[i:i + 1, :]
    return c - c0


def _gates(fr, lb):
    u = jnp.exp(-jnp.abs(fr))
    den = 1.0 / (1.0 + u)
    pos = fr >= 0.0
    f = jnp.where(pos, 1.0 + lb * u, u + lb) * den
    k = (1.0 - lb) * jnp.where(pos, u, 1.0) * den
    return f, k


def _layernorm_silu(y, g, b):
    mu = jnp.mean(y, axis=-1, keepdims=True)
    yc = y - mu
    return _silu(yc * lax.rsqrt(jnp.mean(yc * yc, axis=-1, keepdims=True) + EPS) * g + b)


def _load_cast(jobs, layer, stage, sem):
    slots, rows = stage.shape[0], stage.shape[1]
    chunks = []
    for w_hbm, dst in jobs:
        assert dst.shape[0] % rows == 0 and dst.shape[1] <= stage.shape[2]
        chunks += [(w_hbm, dst, c * rows) for c in range(dst.shape[0] // rows)]

    def copy(i):
        w_hbm, dst, r0 = chunks[i]
        return pltpu.make_async_copy(w_hbm.at[layer, pl.ds(r0, rows), :],
                                     stage.at[i % slots, :, pl.ds(0, dst.shape[1])], sem.at[i % slots])

    for i in range(min(slots - 1, len(chunks))):
        copy(i).start()
    for i, (_, dst, r0) in enumerate(chunks):
        if i + slots - 1 < len(chunks):
            copy(i + slots - 1).start()
        copy(i).wait()
        dst[r0:r0 + rows, :] = stage[i % slots, :, 0:dst.shape[1]].astype(BF16)


def _weight_stage(width):
    return [pltpu.VMEM((STAGE_SLOTS, STAGE_ROWS, width), F32), pltpu.SemaphoreType.DMA((STAGE_SLOTS,))]


def _row_kernel(*refs, layer, n_prompt_tiles, d_ff, do_ple, final_norm):
    it = iter(refs)
    xp_ref, xs_ref = next(it), next(it)
    pp_ref, ps_ref = (next(it), next(it)) if do_ple else (None, None)
    g_ref, wg_hbm, wu_hbm, wd_hbm = next(it), next(it), next(it), next(it)
    gp_ref, wpg_hbm, wpi_hbm = (next(it), next(it), next(it)) if do_ple else (None, None, None)
    gf_ref = next(it) if final_norm else None
    op_ref, os_ref = next(it), next(it)
    wg_ref, wu_ref, wd_ref = next(it), next(it), next(it)
    wpg_ref, wpi_ref = (next(it), next(it)) if do_ple else (None, None)
    stage, sem = next(it), next(it)

    i = pl.program_id(0)

    @pl.when(i == 0)
    def _():
        jobs = [(wg_hbm, wg_ref), (wu_hbm, wu_ref), (wd_hbm, wd_ref)]
        if do_ple:
            jobs += [(wpg_hbm, wpg_ref), (wpi_hbm, wpi_ref)]
        _load_cast(jobs, layer, stage, sem)

    n_tiles = d_ff // MXU_WIDTH
    bounds = [((n_tiles * i) // FF_SPLIT) * MXU_WIDTH for i in range(FF_SPLIT)] + [d_ff]

    inv_rms = lambda x: lax.rsqrt(jnp.mean(x * x, axis=-1, keepdims=True) + EPS)

    def process(x_ref, p_ref, o_ref):
        x = x_ref[...]
        r = inv_rms(x)
        xg = (x * g_ref[...]).astype(BF16)
        acc = None
        for lo, hi in zip(bounds[:-1], bounds[1:]):
            gate = _dot(xg, wg_ref[:, lo:hi]) * r
            up = _dot(xg, wu_ref[:, lo:hi]) * r
            part = _dot((_silu(gate) * up).astype(BF16), wd_ref[lo:hi, :])
            acc = part if acc is None else acc + part
        h = x + 0.5 * acc
        if do_ple:
            gate = jax.nn.sigmoid(_dot((h * gp_ref[...]).astype(BF16), wpg_ref[...]) * inv_rms(h))
            h = h + gate * _dot(p_ref[...].astype(BF16), wpi_ref[...])
        if final_norm:
            h = _rms(h, gf_ref[...])
        o_ref[...] = h

    @pl.when(i < n_prompt_tiles)
    def _():
        process(xp_ref, pp_ref, op_ref)

    @pl.when(i == n_prompt_tiles)
    def _():
        process(xs_ref, ps_ref, os_ref)


def _row_call(hp, hs, layer, g, wg, wu, wd, ple=None, final_gain=None):
    n_p, d = hp.shape
    n_s = hs.shape[0]
    d_ff = wg.shape[-1]
    assert n_p % ROW_TILE == 0 and d_ff % MXU_WIDTH == 0
    npt = n_p // ROW_TILE
    do_ple = ple is not None
    final_norm = final_gain is not None

    ptile = lambda i: (jnp.minimum(i, npt - 1), 0)
    whole = lambda i: (0, 0)
    lsel3 = lambda i: (layer, 0, 0)

    in_specs = [pl.BlockSpec((ROW_TILE, d), ptile), pl.BlockSpec((n_s, d), whole)]
    args = [hp, hs]
    if do_ple:
        pp, ps, gp, wpg, wpi = ple
        d_p = pp.shape[-1]
        in_specs += [pl.BlockSpec((None, ROW_TILE, d_p), lambda i: (layer, jnp.minimum(i, npt - 1), 0)),
                     pl.BlockSpec((None, n_s, d_p), lsel3)]
        args += [pp, ps]
    in_hbm = pl.BlockSpec(memory_space=pl.ANY)
    in_specs += [pl.BlockSpec((None, 1, d), lsel3), in_hbm, in_hbm, in_hbm]
    args += [g, wg, wu, wd]
    scratch = [pltpu.VMEM((d, d_ff), BF16), pltpu.VMEM((d, d_ff), BF16), pltpu.VMEM((d_ff, d), BF16)]
    if do_ple:
        in_specs += [pl.BlockSpec((None, 1, d), lsel3), in_hbm, in_hbm]
        args += [gp, wpg, wpi]
        scratch += [pltpu.VMEM((d, d), BF16), pltpu.VMEM((d_p, d), BF16)]
    if final_norm:
        in_specs += [pl.BlockSpec((1, d), whole)]
        args += [final_gain]
    scratch += _weight_stage(max(d, d_ff))

    return pl.pallas_call(
        functools.partial(_row_kernel, layer=layer, n_prompt_tiles=npt, d_ff=d_ff, do_ple=do_ple,
                          final_norm=final_norm),
        grid=(npt + 1,),
        in_specs=in_specs,
        out_specs=[pl.BlockSpec((ROW_TILE, d), ptile), pl.BlockSpec((n_s, d), whole)],
        out_shape=[jax.ShapeDtypeStruct(hp.shape, F32), jax.ShapeDtypeStruct(hs.shape, F32)],
        scratch_shapes=scratch,
        compiler_params=pltpu.CompilerParams(dimension_semantics=("arbitrary",), vmem_limit_bytes=VMEM_LIMIT),
        name=f"rows_l{layer}_{'ple' if do_ple else 'ffn'}",
    )(*args)


N_LEVELS = CHUNK.bit_length() - 1


def _score_owner():
    t = lax.broadcasted_iota(jnp.int32, (CHUNK, CHUNK), 0)
    s = lax.broadcasted_iota(jnp.int32, (CHUNK, CHUNK), 1)
    return jnp.where(s > t, -1, jnp.where(s == t, N_LEVELS, 31 - lax.clz(t ^ s)))


def _hgrn_scores(q, k, lf, q_adj, owner):
    nb = CHUNK // SUB
    row = lax.broadcasted_iota(jnp.int32, (SUB, D_HEAD), 0)
    bcast = lambda xb, r: jnp.broadcast_to(xb[r:r + 1, :], (SUB, D_HEAD))

    g_blocks, ends = [], []
    carry = None
    for b in range(nb):
        xb = lf[SUB * b:SUB * (b + 1), :]
        for sft in (1, 2, 4):
            xb = xb + jnp.where(row >= sft, pltpu.roll(xb, sft, axis=0), 0.0)
        if carry is not None:
            xb = xb + carry
        carry = bcast(xb, SUB - 1)
        g_blocks.append(xb)
        ends.append(carry)
    g = jnp.concatenate(g_blocks, axis=0)

    def level_factor(level):
        bs = 1 << level
        if bs == 2:
            return [jnp.exp(-jnp.abs(gb - jnp.where(row < 4, bcast(gb, 1), bcast(gb, 5)))) for gb in g_blocks]
        if bs == 4:
            return [jnp.exp(-jnp.abs(gb - bcast(gb, 3))) for gb in g_blocks]
        n = bs // SUB
        out = []
        for b, gb in enumerate(g_blocks):
            ref = ends[(b // (2 * n)) * (2 * n) + n - 1]
            out.append(jnp.exp(gb - ref if (b // n) % 2 == 1 else ref - gb))
        return out

    a = jnp.where(owner == 0, _dot_nt(q_adj, k.astype(BF16)), 0.0)
    for level in range(1, N_LEVELS):
        e = jnp.concatenate(level_factor(level), axis=0)
        a = jnp.where(owner == level, _dot_nt((q * e).astype(BF16), (k * e).astype(BF16)), a)
    a = jnp.where(owner == N_LEVELS, jnp.sum(q * k, axis=-1, keepdims=True), a)

    g_last = jnp.concatenate([ends[nb - 1]] * nb, axis=0)
    q_dec = q * jnp.exp(g)
    k_dec = k * jnp.exp(g_last - g)
    return a.astype(BF16), q_dec.astype(BF16), k_dec.astype(BF16), jnp.exp(ends[nb - 1])


def _mix_prompt_kernel(h_ref, gm_ref, win_hbm, lb_ref, hn_ref, cw_ref, cb_ref, lng_ref, lnb_ref, wout_hbm,
                       *rest, layer, tt, n_prev):
    (o_ref, s_out_ref, c_out_ref,
     q_s, k_s, lf_s, qa_s, v_s, gs_s, oa_s, yc_s, gbuf, st_s, own_s, a_s, qd_s, kd_s, dec_s,
     win_ref, wout_ref, stage, sem) = rest[n_prev:]
    t = pl.program_id(1)

    @pl.when((pl.program_id(0) == 0) & (t == 0))
    def _():
        _load_cast([(win_hbm, win_ref), (wout_hbm, wout_ref)], layer, stage, sem)

    @pl.when(t == 0)
    def _():
        st_s[...] = jnp.zeros_like(st_s)
        gbuf[:, 0:HIST, :] = jnp.zeros((N_SLAB, HIST, D_HEAD), F32)

    own_s[...] = _score_owner()
    chunk_rows = lambda c: pl.ds(pl.multiple_of(c * CHUNK, CHUNK), CHUNK)

    xn = _rms(h_ref[...], gm_ref[...]).astype(BF16)
    group = lambda j: _dot(xn, win_ref[:, j * D_GROUP:(j + 1) * D_GROUP])
    q = _silu(group(0))
    fr = group(1)
    f, k = _gates(fr, _lower_bound(lb_ref, layer))
    odd = (lax.broadcasted_iota(jnp.int32, (tt, D_GROUP), 0) & 1) == 1
    q_s[...] = q
    qa_s[...] = (q * jnp.where(odd, f, 1.0)).astype(BF16)
    k_s[...] = k
    lf_s[...] = jnp.maximum(jnp.log(f), jnp.minimum(fr, 0.0) - LN2)
    v_s[...] = group(2).astype(BF16)
    gs_s[...] = _silu(group(3))
    glu = group(4) * jax.nn.sigmoid(group(5))
    for j in range(N_SLAB):
        gbuf[j, HIST:HIST + tt, :] = glu[:, j * D_HEAD:(j + 1) * D_HEAD]

    def conv_block(i):
        base = pl.multiple_of(i * CONV_BLOCK, CONV_BLOCK)
        for j in range(N_SLAB):
            js = slice(j * D_HEAD, (j + 1) * D_HEAD)
            accs = [jnp.broadcast_to(cb_ref[:, js], (SUB, D_HEAD)) for _ in range(PHASES)]
            for w in range(CONV_W):
                cw = cw_ref[w:w + 1, js]
                for p in range(PHASES):
                    start = base + (HIST - (CONV_W - 1) + p + w)
                    accs[p] = accs[p] + gbuf[j, pl.ds(start, SUB, stride=PHASES), :] * cw
            for p in range(PHASES):
                yc_s[j, pl.ds(base + p, SUB, stride=PHASES), :] = accs[p]

    def stash_scores(c):
        rows = chunk_rows(c)
        owner = own_s[...]
        for h in range(N_HEADS):
            hs = slice(h * D_HEAD, (h + 1) * D_HEAD)
            a_s[h], qd_s[h], kd_s[h], dec_s[h] = _hgrn_scores(q_s[rows, hs], k_s[rows, hs], lf_s[rows, hs],
                                                               qa_s[rows, hs], owner)

    def advance_state(c):
        rows = chunk_rows(c)
        for h in range(N_HEADS):
            hs = slice(h * D_HEAD, (h + 1) * D_HEAD)
            v = v_s[rows, hs]
            st = st_s[h]
            o = _dot_nt(qd_s[h], st.astype(BF16)) + _dot(a_s[h], v)
            st_s[h] = st * dec_s[h][0:1, :] + _dot_tn(v, kd_s[h])
            oa_s[rows, hs] = _rms(o, hn_ref[:, hs]) * gs_s[rows, hs]

    n_chunks = tt // CHUNK
    conv_per_chunk = tt // CONV_BLOCK // n_chunks

    def chunk_body(c, carry):
        advance_state(c - 1)
        stash_scores(c)
        for r in range(conv_per_chunk):
            conv_block(c * conv_per_chunk + r)
        return carry

    stash_scores(0)
    for r in range(conv_per_chunk):
        conv_block(r)
    lax.fori_loop(1, n_chunks, chunk_body, 0)
    advance_state(n_chunks - 1)

    ob = _layernorm_silu(jnp.concatenate([yc_s[j] for j in range(N_SLAB)], axis=-1), lng_ref[...], lnb_ref[...])
    o_ref[...] = (h_ref[...] + _dot(oa_s[...].astype(BF16), wout_ref[0:D_GROUP, :])
                  + _dot(ob.astype(BF16), wout_ref[D_GROUP:2 * D_GROUP, :]))

    @pl.when(t == pl.num_programs(1) - 1)
    def _():
        lo = tt + HIST - (CONV_W - 1)
        c_out_ref[...] = jnp.concatenate([gbuf[j, lo:lo + CONV_W - 1, :] for j in range(N_SLAB)], axis=-1)
        for h in range(N_HEADS):
            s_out_ref[h] = st_s[h].T

    gbuf[:, 0:HIST, :] = gbuf[:, tt:tt + HIST, :]


def _mixer_weight_specs(layer, depth, d, d_in):
    lsel3 = lambda *_: (layer, 0, 0)
    whole2 = lambda *_: (0, 0)
    in_hbm = pl.BlockSpec(memory_space=pl.ANY)
    return [pl.BlockSpec((None, 1, d), lsel3),
            in_hbm,
            pl.BlockSpec((depth, D_GROUP), whole2),
            pl.BlockSpec((None, 1, D_GROUP), lsel3),
            pl.BlockSpec((None, CONV_W, D_GROUP), lsel3),
            pl.BlockSpec((None, 1, D_GROUP), lsel3),
            pl.BlockSpec((None, 1, D_GROUP), lsel3),
            pl.BlockSpec((None, 1, D_GROUP), lsel3),
            in_hbm]


def _mixer_weight_scratch(d, d_in):
    return [pltpu.VMEM((d, d_in), BF16), pltpu.VMEM((2 * D_GROUP, d), BF16)] + _weight_stage(max(d, d_in))


def _stacked_state_outputs(prev, n_args):
    if prev is None:
        return [], [], {}
    return ([pl.BlockSpec(memory_space=pl.ANY)] * len(prev), list(prev),
            {n_args + i: 1 + i for i in range(len(prev))})


def _mix_prompt_call(hp, layer, batch, seq, mixw, prev):
    n_p, d = hp.shape
    depth, _, d_in = mixw[1].shape
    tt = MIX_TILE
    assert seq % tt == 0 and tt % CHUNK == 0 and tt % CONV_BLOCK == 0 and tt >= HIST
    nt = seq // tt
    tile = lambda b, t: (b * nt + t, 0)
    prev_specs, prev_args, aliases = _stacked_state_outputs(prev, 1 + len(mixw))
    scratch = [pltpu.VMEM((tt, D_GROUP), F32),
               pltpu.VMEM((tt, D_GROUP), F32),
               pltpu.VMEM((tt, D_GROUP), F32),
               pltpu.VMEM((tt, D_GROUP), BF16),
               pltpu.VMEM((tt, D_GROUP), BF16),
               pltpu.VMEM((tt, D_GROUP), F32),
               pltpu.VMEM((tt, D_GROUP), F32),
               pltpu.VMEM((N_SLAB, tt, D_HEAD), F32),
               pltpu.VMEM((N_SLAB, tt + HIST, D_HEAD), F32),
               pltpu.VMEM((N_HEADS, D_HEAD, D_HEAD), F32),
               pltpu.VMEM((CHUNK, CHUNK), jnp.int32),
               pltpu.VMEM((N_HEADS, CHUNK, CHUNK), BF16),
               pltpu.VMEM((N_HEADS, CHUNK, D_HEAD), BF16),
               pltpu.VMEM((N_HEADS, CHUNK, D_HEAD), BF16),
               pltpu.VMEM((N_HEADS, SUB, D_HEAD), F32)]
    scratch += _mixer_weight_scratch(d, d_in)
    return pl.pallas_call(
        functools.partial(_mix_prompt_kernel, layer=layer, tt=tt, n_prev=len(prev_args)),
        grid=(batch, nt),
        in_specs=[pl.BlockSpec((tt, d), tile)] + _mixer_weight_specs(layer, depth, d, d_in) + prev_specs,
        out_specs=[pl.BlockSpec((tt, d), tile),
                   pl.BlockSpec((None, None, N_HEADS, D_HEAD, D_HEAD), lambda b, t: (layer, b, 0, 0, 0)),
                   pl.BlockSpec((None, None, CONV_W - 1, D_GROUP), lambda b, t: (layer, b, 0, 0))],
        out_shape=[jax.ShapeDtypeStruct(hp.shape, F32),
                   jax.ShapeDtypeStruct((depth, batch, N_HEADS, D_HEAD, D_HEAD), F32),
                   jax.ShapeDtypeStruct((depth, batch, CONV_W - 1, D_GROUP), F32)],
        scratch_shapes=scratch,
        input_output_aliases=aliases,
        compiler_params=pltpu.CompilerParams(dimension_semantics=("arbitrary", "arbitrary"),
                                             vmem_limit_bytes=VMEM_LIMIT),
        name=f"mix_prompt_l{layer}",
    )(hp, *mixw, *prev_args)


def _mix_sample_kernel(h_ref, gm_ref, win_hbm, lb_ref, hn_ref, cw_ref, cb_ref, lng_ref, lnb_ref, wout_hbm,
                       s_ref, cs_ref, *rest, layer, n_prev):
    (o_ref, s_out_ref, cs_out_ref,
     q_s, f_s, v_s, g_s, glu_s, oraw_s, yc_s, blk_o,
     win_ref, wout_ref, stage, sem) = rest[n_prev:]
    i = pl.program_id(0)
    sb = SAMPLE_BLOCK

    @pl.when(i == 0)
    def _():
        _load_cast([(win_hbm, win_ref), (wout_hbm, wout_ref)], layer, stage, sem)
        proj =_dot(_rms(h_ref[...], gm_ref[...]).astype(BF16), win_ref[...])
        part = lambda j: proj[:, j * D_GROUP:(j + 1) * D_GROUP]
        f, _ = _gates(part(1), _lower_bound(lb_ref, layer))
        q_s[...] = _silu(part(0))
        f_s[...] = f
        v_s[...] = part(2)
        g_s[...] = _silu(part(3))
        glu_s[...] = part(4) * jax.nn.sigmoid(part(5))

    r0 = pl.multiple_of(i * sb, sb)
    rows = pl.ds(r0, sb)
    f8, q8, v8, glu8 = f_s[rows, :], q_s[rows, :], v_s[rows, :], glu_s[rows, :]
    pad = jnp.zeros((D_HEAD - 2 * sb, D_HEAD), F32)
    for h in range(N_HEADS):
        hs = slice(h * D_HEAD, (h + 1) * D_HEAD)
        cols = jnp.concatenate([f8[:, hs], q8[:, hs], pad], axis=0).T
        for j in range(sb):
            vj = v8[j:j + 1, hs]
            sn = vj + cols[:, j:j + 1] * (s_ref[j, h] - vj)
            s_out_ref[j, h] = sn
            blk_o[j:j + 1, hs] = jnp.sum(sn * cols[:, sb + j:sb + j + 1], axis=0, keepdims=True)
    yc = glu8 * cw_ref[CONV_W - 1:CONV_W, :]
    for w in range(CONV_W - 1):
        yc = yc + cs_ref[w] * cw_ref[w:w + 1, :]
    for w in range(CONV_W - 2):
        cs_out_ref[w] = cs_ref[w + 1]
    cs_out_ref[CONV_W - 2] = glu8
    oraw_s[rows, :] = blk_o[...]
    yc_s[rows, :] = yc

    @pl.when(i == pl.num_programs(0) - 1)
    def _():
        oraw = oraw_s[...]
        oa = jnp.concatenate(
            [_rms(oraw[:, h * D_HEAD:(h + 1) * D_HEAD], hn_ref[:, h * D_HEAD:(h + 1) * D_HEAD])
             for h in range(N_HEADS)], axis=-1) * g_s[...]
        ob = _layernorm_silu(yc_s[...] + cb_ref[...], lng_ref[...], lnb_ref[...])
        o_ref[...] = (h_ref[...] + _dot(oa.astype(BF16), wout_ref[0:D_GROUP, :])
                      + _dot(ob.astype(BF16), wout_ref[D_GROUP:2 * D_GROUP, :]))


def _mix_sample_call(hs, layer, state_hgrn, state_conv, mixw, prev):
    n_s, d = hs.shape
    depth, _, d_in = mixw[1].shape
    sb = SAMPLE_BLOCK
    assert n_s % sb == 0 and 2 * sb <= D_HEAD
    whole = lambda i: (0, 0)
    s_spec = pl.BlockSpec((None, sb, N_HEADS, D_HEAD, D_HEAD), lambda i: (layer, i, 0, 0, 0))
    cs_spec = pl.BlockSpec((None, CONV_W - 1, sb, D_GROUP), lambda i: (layer, 0, i, 0))
    prev_specs, prev_args, aliases = _stacked_state_outputs(prev, 3 + len(mixw))
    scratch = [pltpu.VMEM((n_s, D_GROUP), F32) for _ in range(7)]
    scratch += [pltpu.VMEM((sb, D_GROUP), F32)]
    scratch += _mixer_weight_scratch(d, d_in)
    return pl.pallas_call(
        functools.partial(_mix_sample_kernel, layer=layer, n_prev=len(prev_args)),
        grid=(n_s // sb,),
        in_specs=[pl.BlockSpec((n_s, d), whole)] + _mixer_weight_specs(layer, depth, d, d_in)
        + [s_spec, cs_spec] + prev_specs,
        out_specs=[pl.BlockSpec((n_s, d), whole), s_spec, cs_spec],
        out_shape=[jax.ShapeDtypeStruct(hs.shape, F32),
                   jax.ShapeDtypeStruct(state_hgrn.shape, F32),
                   jax.ShapeDtypeStruct(state_conv.shape, F32)],
        scratch_shapes=scratch,
        input_output_aliases=aliases,
        compiler_params=pltpu.CompilerParams(dimension_semantics=("arbitrary",), vmem_limit_bytes=VMEM_LIMIT),
        name=f"mix_sample_l{layer}",
    )(hs, *mixw, state_hgrn, state_conv, *prev_args)


def kernel(x_prompt, x_sample, state_hgrn, state_conv, p_prompt, p_sample, norm_ffn1, ffn1_w_gate, ffn1_w_up, ffn1_w_down, norm_mix, w_in, hgrn_lb, hgrn_norm, conv_w, conv_b, conv_ln_g, conv_ln_b, w_out, norm_ffn2, ffn2_w_gate, ffn2_w_up, ffn2_w_down, norm_ple, w_ple_gate, w_ple_in, norm_final):
    batch, seq, d = x_prompt.shape
    n_s = x_sample.shape[0] * x_sample.shape[1]
    depth = w_in.shape[0]
    assert w_in.shape[-1] == 6 * D_GROUP and hgrn_lb.shape == (depth, D_GROUP)
    assert state_hgrn.shape[2:] == (N_HEADS, D_HEAD, D_HEAD) and conv_w.shape[1:] == (CONV_W, D_GROUP)

    row3 = lambda a: a.reshape(depth, 1, a.shape[-1])
    ffn1 = (row3(norm_ffn1), ffn1_w_gate, ffn1_w_up, ffn1_w_down)
    ffn2 = (row3(norm_ffn2), ffn2_w_gate, ffn2_w_up, ffn2_w_down)
    mixw = (row3(norm_mix), w_in, hgrn_lb, row3(hgrn_norm), conv_w, row3(conv_b), row3(conv_ln_g),
            row3(conv_ln_b), w_out)
    pp = p_prompt.reshape(depth, batch * seq, p_prompt.shape[-1])
    ps = p_sample.reshape(depth, n_s, p_sample.shape[-1])
    ple = (pp, ps, row3(norm_ple), w_ple_gate, w_ple_in)
    gain_final = norm_final.reshape(1, d)
    conv_taps = jnp.swapaxes(state_conv, 1, 2)

    hp = x_prompt.reshape(batch * seq, d)
    hs = x_sample.reshape(n_s, d)
    new_p = new_s = None
    for layer in range(depth):
        hp, hs = _row_call(hp, hs, layer, *ffn1)
        hp, *new_p = _mix_prompt_call(hp, layer, batch, seq, mixw, new_p)
        hs, *new_s = _mix_sample_call(hs, layer, state_hgrn, conv_taps, mixw, new_s)
        hp, hs = _row_call(hp, hs, layer, *ffn2, ple=ple, final_gain=gain_final if layer == depth - 1 else None)
    return (hp.reshape(x_prompt.shape), hs.reshape(x_sample.shape), new_p[0], new_p[1], new_s[0],
            jnp.swapaxes(new_s[1], 1, 2))
```

```python
import functools

import jax
import jax.numpy as jnp
from jax import lax
from jax.experimental import pallas as pl
from jax.experimental.pallas import tpu as pltpu

F32 = jnp.float32
BF16 = jnp.bfloat16

EPS = 1e-6
LN2 = 0.6931471805599453
N_HEADS = 4
D_HEAD = 128
MXU_WIDTH = 256
D_GROUP = 512
N_SLAB = D_GROUP // D_HEAD
CONV_W = 31
CHUNK = 64
SUB = 8
HIST = 32
PHASES = 4
CONV_BLOCK = PHASES * SUB

ROW_TILE = 512
MIX_TILE = 512
FF_SPLIT = 2
STAGE_ROWS = 256
STAGE_SLOTS = 3
SAMPLE_BLOCK = 16
VMEM_LIMIT = 56 * 1024 * 1024


def _dot(a, b):
    return jnp.dot(a, b, preferred_element_type=F32)


def _dot_nt(a, b):
    return lax.dot_general(a, b, (((1,), (1,)), ((), ())), preferred_element_type=F32)


def _dot_tn(a, b):
    return lax.dot_general(a, b, (((0,), (0,)), ((), ())), preferred_element_type=F32)


def _rms(x, g):
    return x * lax.rsqrt(jnp.mean(x * x, axis=-1, keepdims=True) + EPS) * g


def _silu(x):
    return x * jax.nn.sigmoid(x)


def _lower_bound(lb_ref, layer):
    z = lb_ref[...]
    e = jnp.exp(z - jnp.max(z, axis=0, keepdims=True))
    sm = e / jnp.sum(e, axis=0, keepdims=True)
    c0 = sm[0:1, :]
    c = c0
    for i in range(1, layer + 1):
        c = c + sm[i:i + 1, :]
    return c - c0


def _gates(fr, lb):
    u = jnp.exp(-jnp.abs(fr))
    den = 1.0 / (1.0 + u)
    pos = fr >= 0.0
    f = jnp.where(pos, 1.0 + lb * u, u + lb) * den
    k = (1.0 - lb) * jnp.where(pos, u, 1.0) * den
    return f, k


def _layernorm_silu(y, g, b):
    mu = jnp.mean(y, axis=-1, keepdims=True)
    yc = y - mu
    return _silu(yc * lax.rsqrt(jnp.mean(yc * yc, axis=-1, keepdims=True) + EPS) * g + b)


def _load_cast(jobs, layer, stage, sem):
    slots, rows = stage.shape[0], stage.shape[1]
    chunks = []
    for w_hbm, dst in jobs:
        assert dst.shape[0] % rows == 0 and dst.shape[1] <= stage.shape[2]
        chunks += [(w_hbm, dst, c * rows) for c in range(dst.shape[0] // rows)]

    def copy(i):
        w_hbm, dst, r0 = chunks[i]
        return pltpu.make_async_copy(w_hbm.at[layer, pl.ds(r0, rows), :],
                                     stage.at[i % slots, :, pl.ds(0, dst.shape[1])], sem.at[i % slots])

    for i in range(min(slots - 1, len(chunks))):
        copy(i).start()
    for i, (_, dst, r0) in enumerate(chunks):
        if i + slots - 1 < len(chunks):
            copy(i + slots - 1).start()
        copy(i).wait()
        dst[r0:r0 + rows, :] = stage[i % slots, :, 0:dst.shape[1]].astype(BF16)


def _weight_stage(width):
    return [pltpu.VMEM((STAGE_SLOTS, STAGE_ROWS, width), F32), pltpu.SemaphoreType.DMA((STAGE_SLOTS,))]


def _row_kernel(*refs, layer, n_prompt_tiles, d_ff, do_ple, final_norm):
    it = iter(refs)
    xp_ref, xs_ref = next(it), next(it)
    pp_ref, ps_ref = (next(it), next(it)) if do_ple else (None, None)
    g_ref, wg_hbm, wu_hbm, wd_hbm = next(it), next(it), next(it), next(it)
    gp_ref, wpg_hbm, wpi_hbm = (next(it), next(it), next(it)) if do_ple else (None, None, None)
    gf_ref = next(it) if final_norm else None
    op_ref, os_ref = next(it), next(it)
    wg_ref, wu_ref, wd_ref = next(it), next(it), next(it)
    wpg_ref, wpi_ref = (next(it), next(it)) if do_ple else (None, None)
    stage, sem = next(it), next(it)

    i = pl.program_id(0)

    @pl.when(i == 0)
    def _():
        jobs = [(wg_hbm, wg_ref), (wu_hbm, wu_ref), (wd_hbm, wd_ref)]
        if do_ple:
            jobs += [(wpg_hbm, wpg_ref), (wpi_hbm, wpi_ref)]
        _load_cast(jobs, layer, stage, sem)

    n_tiles = d_ff // MXU_WIDTH
    bounds = [((n_tiles * i) // FF_SPLIT) * MXU_WIDTH for i in range(FF_SPLIT)] + [d_ff]

    inv_rms = lambda x: lax.rsqrt(jnp.mean(x * x, axis=-1, keepdims=True) + EPS)

    def process(x_ref, p_ref, o_ref):
        x = x_ref[...]
        r = inv_rms(x)
        xg = (x * g_ref[...]).astype(BF16)
        acc = None
        for lo, hi in zip(bounds[:-1], bounds[1:]):
            gate = _dot(xg, wg_ref[:, lo:hi]) * r
            up = _dot(xg, wu_ref[:, lo:hi]) * r
            part = _dot((_silu(gate) * up).astype(BF16), wd_ref[lo:hi, :])
            acc = part if acc is None else acc + part
        h = x + 0.5 * acc
        if do_ple:
            gate = jax.nn.sigmoid(_dot((h * gp_ref[...]).astype(BF16), wpg_ref[...]) * inv_rms(h))
            h = h + gate * _dot(p_ref[...].astype(BF16), wpi_ref[...])
        if final_norm:
            h = _rms(h, gf_ref[...])
        o_ref[...] = h

    @pl.when(i < n_prompt_tiles)
    def _():
        process(xp_ref, pp_ref, op_ref)

    @pl.when(i == n_prompt_tiles)
    def _():
        process(xs_ref, ps_ref, os_ref)


def _row_call(hp, hs, layer, g, wg, wu, wd, ple=None, final_gain=None):
    n_p, d = hp.shape
    n_s = hs.shape[0]
    d_ff = wg.shape[-1]
    assert n_p % ROW_TILE == 0 and d_ff % MXU_WIDTH == 0
    npt = n_p // ROW_TILE
    do_ple = ple is not None
    final_norm = final_gain is not None

    ptile = lambda i: (jnp.minimum(i, npt - 1), 0)
    whole = lambda i: (0, 0)
    lsel3 = lambda i: (layer, 0, 0)

    in_specs = [pl.BlockSpec((ROW_TILE, d), ptile), pl.BlockSpec((n_s, d), whole)]
    args = [hp, hs]
    if do_ple:
        pp, ps, gp, wpg, wpi = ple
        d_p = pp.shape[-1]
        in_specs += [pl.BlockSpec((None, ROW_TILE, d_p), lambda i: (layer, jnp.minimum(i, npt - 1), 0)),
                     pl.BlockSpec((None, n_s, d_p), lsel3)]
        args += [pp, ps]
    in_hbm = pl.BlockSpec(memory_space=pl.ANY)
    in_specs += [pl.BlockSpec((None, 1, d), lsel3), in_hbm, in_hbm, in_hbm]
    args += [g, wg, wu, wd]
    scratch = [pltpu.VMEM((d, d_ff), BF16), pltpu.VMEM((d, d_ff), BF16), pltpu.VMEM((d_ff, d), BF16)]
    if do_ple:
        in_specs += [pl.BlockSpec((None, 1, d), lsel3), in_hbm, in_hbm]
        args += [gp, wpg, wpi]
        scratch += [pltpu.VMEM((d, d), BF16), pltpu.VMEM((d_p, d), BF16)]
    if final_norm:
        in_specs += [pl.BlockSpec((1, d), whole)]
        args += [final_gain]
    scratch += _weight_stage(max(d, d_ff))

    return pl.pallas_call(
        functools.partial(_row_kernel, layer=layer, n_prompt_tiles=npt, d_ff=d_ff, do_ple=do_ple,
                          final_norm=final_norm),
        grid=(npt + 1,),
        in_specs=in_specs,
        out_specs=[pl.BlockSpec((ROW_TILE, d), ptile), pl.BlockSpec((n_s, d), whole)],
        out_shape=[jax.ShapeDtypeStruct(hp.shape, F32), jax.ShapeDtypeStruct(hs.shape, F32)],
        scratch_shapes=scratch,
        compiler_params=pltpu.CompilerParams(dimension_semantics=("arbitrary",), vmem_limit_bytes=VMEM_LIMIT),
        name=f"rows_l{layer}_{'ple' if do_ple else 'ffn'}",
    )(*args)


N_LEVELS = CHUNK.bit_length() - 1


def _score_owner():
    t = lax.broadcasted_iota(jnp.int32, (CHUNK, CHUNK), 0)
    s = lax.broadcasted_iota(jnp.int32, (CHUNK, CHUNK), 1)
    return jnp.where(s > t, -1, jnp.where(s == t, N_LEVELS, 31 - lax.clz(t ^ s)))


def _hgrn_scores(q, k, lf, q_adj, owner):
    nb = CHUNK // SUB
    row = lax.broadcasted_iota(jnp.int32, (SUB, D_HEAD), 0)
    bcast = lambda xb, r: jnp.broadcast_to(xb[r:r + 1, :], (SUB, D_HEAD))

    g_blocks, ends = [], []
    carry = None
    for b in range(nb):
        xb = lf[SUB * b:SUB * (b + 1), :]
        for sft in (1, 2, 4):
            xb = xb + jnp.where(row >= sft, pltpu.roll(xb, sft, axis=0), 0.0)
        if carry is not None:
            xb = xb + carry
        carry = bcast(xb, SUB - 1)
        g_blocks.append(xb)
        ends.append(carry)
    g = jnp.concatenate(g_blocks, axis=0)

    def level_factor(level):
        bs = 1 << level
        if bs == 2:
            return [jnp.exp(-jnp.abs(gb - jnp.where(row < 4, bcast(gb, 1), bcast(gb, 5)))) for gb in g_blocks]
        if bs == 4:
            return [jnp.exp(-jnp.abs(gb - bcast(gb, 3))) for gb in g_blocks]
        n = bs // SUB
        out = []
        for b, gb in enumerate(g_blocks):
            ref = ends[(b // (2 * n)) * (2 * n) + n - 1]
            out.append(jnp.exp(gb - ref if (b // n) % 2 == 1 else ref - gb))
        return out

    q16, k16 = q.astype(BF16), k.astype(BF16)
    a = jnp.where(owner == 0, _dot_nt(q_adj, k16), 0.0)
    for level in range(1, N_LEVELS):
        e = jnp.concatenate(level_factor(level), axis=0).astype(BF16)
        a = jnp.where(owner == level, _dot_nt(q16 * e, k16 * e), a)
    a = jnp.where(owner == N_LEVELS, jnp.sum(q * k, axis=-1, keepdims=True), a)

    g_last = jnp.concatenate([ends[nb - 1]] * nb, axis=0)
    q_dec = q * jnp.exp(g)
    k_dec = k * jnp.exp(g_last - g)
    return a.astype(BF16), q_dec.astype(BF16), k_dec.astype(BF16), jnp.exp(ends[nb - 1])


def _mix_prompt_kernel(h_ref, gm_ref, win_hbm, lb_ref, hn_ref, cw_ref, cb_ref, lng_ref, lnb_ref, wout_hbm,
                       *rest, layer, tt, n_prev):
    (o_ref, s_out_ref, c_out_ref,
     q_s, k_s, lf_s, qa_s, v_s, gs_s, oa_s, yc_s, gbuf, st_s, own_s, a_s, qd_s, kd_s, dec_s,
     win_ref, wout_ref, stage, sem) = rest[n_prev:]
    t = pl.program_id(1)

    @pl.when((pl.program_id(0) == 0) & (t == 0))
    def _():
        _load_cast([(win_hbm, win_ref), (wout_hbm, wout_ref)], layer, stage, sem)

    @pl.when(t == 0)
    def _():
        st_s[...] = jnp.zeros_like(st_s)
        gbuf[:, 0:HIST, :] = jnp.zeros((N_SLAB, HIST, D_HEAD), F32)

    own_s[...] = _score_owner()
    chunk_rows = lambda c: pl.ds(pl.multiple_of(c * CHUNK, CHUNK), CHUNK)

    xn = _rms(h_ref[...], gm_ref[...]).astype(BF16)
    group = lambda j: _dot(xn, win_ref[:, j * D_GROUP:(j + 1) * D_GROUP])
    q = _silu(group(0))
    fr = group(1)
    f, k = _gates(fr, _lower_bound(lb_ref, layer))
    odd = (lax.broadcasted_iota(jnp.int32, (tt, D_GROUP), 0) & 1) == 1
    q_s[...] = q
    qa_s[...] = (q * jnp.where(odd, f, 1.0)).astype(BF16)
    k_s[...] = k
    lf_s[...] = jnp.maximum(jnp.log(f), jnp.minimum(fr, 0.0) - LN2)
    v_s[...] = group(2).astype(BF16)
    gs_s[...] = _silu(group(3))
    glu = group(4) * jax.nn.sigmoid(group(5))
    for j in range(N_SLAB):
        gbuf[j, HIST:HIST + tt, :] = glu[:, j * D_HEAD:(j + 1) * D_HEAD]

    def conv_block(i):
        base = pl.multiple_of(i * CONV_BLOCK, CONV_BLOCK)
        for j in range(N_SLAB):
            js = slice(j * D_HEAD, (j + 1) * D_HEAD)
            accs = [jnp.broadcast_to(cb_ref[:, js], (SUB, D_HEAD)) for _ in range(PHASES)]
            for w in range(CONV_W):
                cw = cw_ref[w:w + 1, js]
                for p in range(PHASES):
                    start = base + (HIST - (CONV_W - 1) + p + w)
                    accs[p] = accs[p] + gbuf[j, pl.ds(start, SUB, stride=PHASES), :] * cw
            for p in range(PHASES):
                yc_s[j, pl.ds(base + p, SUB, stride=PHASES), :] = accs[p]

    def stash_scores(c):
        rows = chunk_rows(c)
        owner = own_s[...]
        for h in range(N_HEADS):
            hs = slice(h * D_HEAD, (h + 1) * D_HEAD)
            a_s[h], qd_s[h], kd_s[h], dec_s[h] = _hgrn_scores(q_s[rows, hs], k_s[rows, hs], lf_s[rows, hs],
                                                               qa_s[rows, hs], owner)

    def advance_state(c):
        rows = chunk_rows(c)
        for h in range(N_HEADS):
            hs = slice(h * D_HEAD, (h + 1) * D_HEAD)
            v = v_s[rows, hs]
            st = st_s[h]
            o = _dot_nt(qd_s[h], st.astype(BF16)) + _dot(a_s[h], v)
            st_s[h] = st * dec_s[h][0:1, :] + _dot_tn(v, kd_s[h])
            oa_s[rows, hs] = _rms(o, hn_ref[:, hs]) * gs_s[rows, hs]

    n_chunks = tt // CHUNK
    conv_per_chunk = tt // CONV_BLOCK // n_chunks

    def chunk_body(c, carry):
        advance_state(c - 1)
        stash_scores(c)
        for r in range(conv_per_chunk):
            conv_block(c * conv_per_chunk + r)
        return carry

    stash_scores(0)
    for r in range(conv_per_chunk):
        conv_block(r)
    lax.fori_loop(1, n_chunks, chunk_body, 0)
    advance_state(n_chunks - 1)

    ob = _layernorm_silu(jnp.concatenate([yc_s[j] for j in range(N_SLAB)], axis=-1), lng_ref[...], lnb_ref[...])
    o_ref[...] = (h_ref[...] + _dot(oa_s[...].astype(BF16), wout_ref[0:D_GROUP, :])
                  + _dot(ob.astype(BF16), wout_ref[D_GROUP:2 * D_GROUP, :]))

    @pl.when(t == pl.num_programs(1) - 1)
    def _():
        lo = tt + HIST - (CONV_W - 1)
        c_out_ref[...] = jnp.concatenate([gbuf[j, lo:lo + CONV_W - 1, :] for j in range(N_SLAB)], axis=-1)
        for h in range(N_HEADS):
            s_out_ref[h] = st_s[h].T

    gbuf[:, 0:HIST, :] = gbuf[:, tt:tt + HIST, :]


def _mixer_weight_specs(layer, depth, d, d_in):
    lsel3 = lambda *_: (layer, 0, 0)
    whole2 = lambda *_: (0, 0)
    in_hbm = pl.BlockSpec(memory_space=pl.ANY)
    return [pl.BlockSpec((None, 1, d), lsel3),
            in_hbm,
            pl.BlockSpec((depth, D_GROUP), whole2),
            pl.BlockSpec((None, 1, D_GROUP), lsel3),
            pl.BlockSpec((None, CONV_W, D_GROUP), lsel3),
            pl.BlockSpec((None, 1, D_GROUP), lsel3),
            pl.BlockSpec((None, 1, D_GROUP), lsel3),
            pl.BlockSpec((None, 1, D_GROUP), lsel3),
            in_hbm]


def _mixer_weight_scratch(d, d_in):
    return [pltpu.VMEM((d, d_in), BF16), pltpu.VMEM((2 * D_GROUP, d), BF16)] + _weight_stage(max(d, d_in))


def _stacked_state_outputs(prev, n_args):
    if prev is None:
        return [], [], {}
    return ([pl.BlockSpec(memory_space=pl.ANY)] * len(prev), list(prev),
            {n_args + i: 1 + i for i in range(len(prev))})


def _mix_prompt_call(hp, layer, batch, seq, mixw, prev):
    n_p, d = hp.shape
    depth, _, d_in = mixw[1].shape
    tt = MIX_TILE
    assert seq % tt == 0 and tt % CHUNK == 0 and tt % CONV_BLOCK == 0 and tt >= HIST
    nt = seq // tt
    tile = lambda b, t: (b * nt + t, 0)
    prev_specs, prev_args, aliases = _stacked_state_outputs(prev, 1 + len(mixw))
    scratch = [pltpu.VMEM((tt, D_GROUP), F32),
               pltpu.VMEM((tt, D_GROUP), F32),
               pltpu.VMEM((tt, D_GROUP), F32),
               pltpu.VMEM((tt, D_GROUP), BF16),
               pltpu.VMEM((tt, D_GROUP), BF16),
               pltpu.VMEM((tt, D_GROUP), F32),
               pltpu.VMEM((tt, D_GROUP), F32),
               pltpu.VMEM((N_SLAB, tt, D_HEAD), F32),
               pltpu.VMEM((N_SLAB, tt + HIST, D_HEAD), F32),
               pltpu.VMEM((N_HEADS, D_HEAD, D_HEAD), F32),
               pltpu.VMEM((CHUNK, CHUNK), jnp.int32),
               pltpu.VMEM((N_HEADS, CHUNK, CHUNK), BF16),
               pltpu.VMEM((N_HEADS, CHUNK, D_HEAD), BF16),
               pltpu.VMEM((N_HEADS, CHUNK, D_HEAD), BF16),
               pltpu.VMEM((N_HEADS, SUB, D_HEAD), F32)]
    scratch += _mixer_weight_scratch(d, d_in)
    return pl.pallas_call(
        functools.partial(_mix_prompt_kernel, layer=layer, tt=tt, n_prev=len(prev_args)),
        grid=(batch, nt),
        in_specs=[pl.BlockSpec((tt, d), tile)] + _mixer_weight_specs(layer, depth, d, d_in) + prev_specs,
        out_specs=[pl.BlockSpec((tt, d), tile),
                   pl.BlockSpec((None, None, N_HEADS, D_HEAD, D_HEAD), lambda b, t: (layer, b, 0, 0, 0)),
                   pl.BlockSpec((None, None, CONV_W - 1, D_GROUP), lambda b, t: (layer, b, 0, 0))],
        out_shape=[jax.ShapeDtypeStruct(hp.shape, F32),
                   jax.ShapeDtypeStruct((depth, batch, N_HEADS, D_HEAD, D_HEAD), F32),
                   jax.ShapeDtypeStruct((depth, batch, CONV_W - 1, D_GROUP), F32)],
        scratch_shapes=scratch,
        input_output_aliases=aliases,
        compiler_params=pltpu.CompilerParams(dimension_semantics=("arbitrary", "arbitrary"),
                                             vmem_limit_bytes=VMEM_LIMIT),
        name=f"mix_prompt_l{layer}",
    )(hp, *mixw, *prev_args)


def _mix_sample_kernel(h_ref, gm_ref, win_hbm, lb_ref, hn_ref, cw_ref, cb_ref, lng_ref, lnb_ref, wout_hbm,
                       s_ref, cs_ref, *rest, layer, n_prev):
    (o_ref, s_out_ref, cs_out_ref,
     q_s, f_s, v_s, g_s, glu_s, oraw_s, yc_s, blk_o,
     win_ref, wout_ref, stage, sem) = rest[n_prev:]
    i = pl.program_id(0)
    sb = SAMPLE_BLOCK

    @pl.when(i == 0)
    def _():
        _load_cast([(win_hbm, win_ref), (wout_hbm, wout_ref)], layer, stage, sem)
        proj =_dot(_rms(h_ref[...], gm_ref[...]).astype(BF16), win_ref[...])
        part = lambda j: proj[:, j * D_GROUP:(j + 1) * D_GROUP]
        f, _ = _gates(part(1), _lower_bound(lb_ref, layer))
        q_s[...] = _silu(part(0))
        f_s[...] = f
        v_s[...] = part(2)
        g_s[...] = _silu(part(3))
        glu_s[...] = part(4) * jax.nn.sigmoid(part(5))

    r0 = pl.multiple_of(i * sb, sb)
    rows = pl.ds(r0, sb)
    f8, q8, v8, glu8 = f_s[rows, :], q_s[rows, :], v_s[rows, :], glu_s[rows, :]
    pad = jnp.zeros((D_HEAD - 2 * sb, D_HEAD), F32)
    for h in range(N_HEADS):
        hs = slice(h * D_HEAD, (h + 1) * D_HEAD)
        cols = jnp.concatenate([f8[:, hs], q8[:, hs], pad], axis=0).T
        for j in range(sb):
            vj = v8[j:j + 1, hs]
            sn = vj + cols[:, j:j + 1] * (s_ref[j, h] - vj)
            s_out_ref[j, h] = sn
            blk_o[j:j + 1, hs] = jnp.sum(sn * cols[:, sb + j:sb + j + 1], axis=0, keepdims=True)
    yc = glu8 * cw_ref[CONV_W - 1:CONV_W, :]
    for w in range(CONV_W - 1):
        yc = yc + cs_ref[w] * cw_ref[w:w + 1, :]
    for w in range(CONV_W - 2):
        cs_out_ref[w] = cs_ref[w + 1]
    cs_out_ref[CONV_W - 2] = glu8
    oraw_s[rows, :] = blk_o[...]
    yc_s[rows, :] = yc

    @pl.when(i == pl.num_programs(0) - 1)
    def _():
        oraw = oraw_s[...]
        oa = jnp.concatenate(
            [_rms(oraw[:, h * D_HEAD:(h + 1) * D_HEAD], hn_ref[:, h * D_HEAD:(h + 1) * D_HEAD])
             for h in range(N_HEADS)], axis=-1) * g_s[...]
        ob = _layernorm_silu(yc_s[...] + cb_ref[...], lng_ref[...], lnb_ref[...])
        o_ref[...] = (h_ref[...] + _dot(oa.astype(BF16), wout_ref[0:D_GROUP, :])
                      + _dot(ob.astype(BF16), wout_ref[D_GROUP:2 * D_GROUP, :]))


def _mix_sample_call(hs, layer, state_hgrn, state_conv, mixw, prev):
    n_s, d = hs.shape
    depth, _, d_in = mixw[1].shape
    sb = SAMPLE_BLOCK
    assert n_s % sb == 0 and 2 * sb <= D_HEAD
    whole = lambda i: (0, 0)
    s_spec = pl.BlockSpec((None, sb, N_HEADS, D_HEAD, D_HEAD), lambda i: (layer, i, 0, 0, 0))
    cs_spec = pl.BlockSpec((None, CONV_W - 1, sb, D_GROUP), lambda i: (layer, 0, i, 0))
    prev_specs, prev_args, aliases = _stacked_state_outputs(prev, 3 + len(mixw))
    scratch = [pltpu.VMEM((n_s, D_GROUP), F32) for _ in range(7)]
    scratch += [pltpu.VMEM((sb, D_GROUP), F32)]
    scratch += _mixer_weight_scratch(d, d_in)
    return pl.pallas_call(
        functools.partial(_mix_sample_kernel, layer=layer, n_prev=len(prev_args)),
        grid=(n_s // sb,),
        in_specs=[pl.BlockSpec((n_s, d), whole)] + _mixer_weight_specs(layer, depth, d, d_in)
        + [s_spec, cs_spec] + prev_specs,
        out_specs=[pl.BlockSpec((n_s, d), whole), s_spec, cs_spec],
        out_shape=[jax.ShapeDtypeStruct(hs.shape, F32),
                   jax.ShapeDtypeStruct(state_hgrn.shape, F32),
                   jax.ShapeDtypeStruct(state_conv.shape, F32)],
        scratch_shapes=scratch,
        input_output_aliases=aliases,
        compiler_params=pltpu.CompilerParams(dimension_semantics=("arbitrary",), vmem_limit_bytes=VMEM_LIMIT),
        name=f"mix_sample_l{layer}",
    )(hs, *mixw, state_hgrn, state_conv, *prev_args)


def kernel(x_prompt, x_sample, state_hgrn, state_conv, p_prompt, p_sample, norm_ffn1, ffn1_w_gate, ffn1_w_up, ffn1_w_down, norm_mix, w_in, hgrn_lb, hgrn_norm, conv_w, conv_b, conv_ln_g, conv_ln_b, w_out, norm_ffn2, ffn2_w_gate, ffn2_w_up, ffn2_w_down, norm_ple, w_ple_gate, w_ple_in, norm_final):
    batch, seq, d = x_prompt.shape
    n_s = x_sample.shape[0] * x_sample.shape[1]
    depth = w_in.shape[0]
    assert w_in.shape[-1] == 6 * D_GROUP and hgrn_lb.shape == (depth, D_GROUP)
    assert state_hgrn.shape[2:] == (N_HEADS, D_HEAD, D_HEAD) and conv_w.shape[1:] == (CONV_W, D_GROUP)

    row3 = lambda a: a.reshape(depth, 1, a.shape[-1])
    ffn1 = (row3(norm_ffn1), ffn1_w_gate, ffn1_w_up, ffn1_w_down)
    ffn2 = (row3(norm_ffn2), ffn2_w_gate, ffn2_w_up, ffn2_w_down)
    mixw = (row3(norm_mix), w_in, hgrn_lb, row3(hgrn_norm), conv_w, row3(conv_b), row3(conv_ln_g),
            row3(conv_ln_b), w_out)
    pp = p_prompt.reshape(depth, batch * seq, p_prompt.shape[-1])
    ps = p_sample.reshape(depth, n_s, p_sample.shape[-1])
    ple = (pp, ps, row3(norm_ple), w_ple_gate, w_ple_in)
    gain_final = norm_final.reshape(1, d)
    conv_taps = jnp.swapaxes(state_conv, 1, 2)

    hp = x_prompt.reshape(batch * seq, d)
    hs = x_sample.reshape(n_s, d)
    new_p = new_s = None
    for layer in range(depth):
        hp, hs = _row_call(hp, hs, layer, *ffn1)
        hp, *new_p = _mix_prompt_call(hp, layer, batch, seq, mixw, new_p)
        hs, *new_s = _mix_sample_call(hs, layer, state_hgrn, conv_taps, mixw, new_s)
        hp, hs = _row_call(hp, hs, layer, *ffn2, ple=ple, final_gain=gain_final if layer == depth - 1 else None)
    return (hp.reshape(x_prompt.shape), hs.reshape(x_sample.shape), new_p[0], new_p[1], new_s[0],
            jnp.swapaxes(new_s[1], 1, 2))
```

```python
import functools

import jax
import jax.numpy as jnp
from jax import lax
from jax.experimental import pallas as pl
from jax.experimental.pallas import tpu as pltpu

F32 = jnp.float32
BF16 = jnp.bfloat16

EPS = 1e-6
LN2 = 0.6931471805599453
N_HEADS = 4
D_HEAD = 128
MXU_WIDTH = 256
D_GROUP = 512
N_SLAB = D_GROUP // D_HEAD
CONV_W = 31
CHUNK = 64
SUB = 8
HIST = 32
PHASES = 4
CONV_BLOCK = PHASES * SUB

ROW_TILE = 512
MIX_TILE = 1024
FF_SPLIT = 2
STAGE_ROWS = 256
STAGE_SLOTS = 3
SAMPLE_BLOCK = 16
VMEM_LIMIT = 60 * 1024 * 1024


def _dot(a, b):
    return jnp.dot(a, b, preferred_element_type=F32)


def _dot_nt(a, b):
    return lax.dot_general(a, b, (((1,), (1,)), ((), ())), preferred_element_type=F32)


def _dot_tn(a, b):
    return lax.dot_general(a, b, (((0,), (0,)), ((), ())), preferred_element_type=F32)


def _rms(x, g):
    return x * lax.rsqrt(jnp.mean(x * x, axis=-1, keepdims=True) + EPS) * g


def _silu(x):
    return x * jax.nn.sigmoid(x)


def _lower_bound(lb_ref, layer):
    z = lb_ref[...]
    e = jnp.exp(z - jnp.max(z, axis=0, keepdims=True))
    sm = e / jnp.sum(e, axis=0, keepdims=True)
    c0 = sm[0:1, :]
    c = c0
    for i in range(1, layer + 1):
        c = c + sm[i:i + 1, :]
    return c - c0


def _gates(fr, lb):
    u = jnp.exp(-jnp.abs(fr))
    den = 1.0 / (1.0 + u)
    pos = fr >= 0.0
    f = jnp.where(pos, 1.0 + lb * u, u + lb) * den
    k = (1.0 - lb) * jnp.where(pos, u, 1.0) * den
    return f, k


def _layernorm_silu(y, g, b):
    mu = jnp.mean(y, axis=-1, keepdims=True)
    yc = y - mu
    return _silu(yc * lax.rsqrt(jnp.mean(yc * yc, axis=-1, keepdims=True) + EPS) * g + b)


def _load_cast(jobs, layer, stage, sem):
    slots, rows = stage.shape[0], stage.shape[1]
    chunks = []
    for w_hbm, dst in jobs:
        assert dst.shape[0] % rows == 0 and dst.shape[1] <= stage.shape[2]
        chunks += [(w_hbm, dst, c * rows) for c in range(dst.shape[0] // rows)]

    def copy(i):
        w_hbm, dst, r0 = chunks[i]
        return pltpu.make_async_copy(w_hbm.at[layer, pl.ds(r0, rows), :],
                                     stage.at[i % slots, :, pl.ds(0, dst.shape[1])], sem.at[i % slots])

    for i in range(min(slots - 1, len(chunks))):
        copy(i).start()
    for i, (_, dst, r0) in enumerate(chunks):
        if i + slots - 1 < len(chunks):
            copy(i + slots - 1).start()
        copy(i).wait()
        dst[r0:r0 + rows, :] = stage[i % slots, :, 0:dst.shape[1]].astype(BF16)


def _weight_stage(width):
    return [pltpu.VMEM((STAGE_SLOTS, STAGE_ROWS, width), F32), pltpu.SemaphoreType.DMA((STAGE_SLOTS,))]


def _row_kernel(*refs, layer, n_prompt_tiles, d_ff, do_ple, final_norm):
    it = iter(refs)
    xp_ref, xs_ref = next(it), next(it)
    pp_ref, ps_ref = (next(it), next(it)) if do_ple else (None, None)
    g_ref, wg_hbm, wu_hbm, wd_hbm = next(it), next(it), next(it), next(it)
    gp_ref, wpg_hbm, wpi_hbm = (next(it), next(it), next(it)) if do_ple else (None, None, None)
    gf_ref = next(it) if final_norm else None
    op_ref, os_ref = next(it), next(it)
    wg_ref, wu_ref, wd_ref = next(it), next(it), next(it)
    wpg_ref, wpi_ref = (next(it), next(it)) if do_ple else (None, None)
    stage, sem = next(it), next(it)

    i = pl.program_id(0)

    @pl.when(i == 0)
    def _():
        jobs = [(wg_hbm, wg_ref), (wu_hbm, wu_ref), (wd_hbm, wd_ref)]
        if do_ple:
            jobs += [(wpg_hbm, wpg_ref), (wpi_hbm, wpi_ref)]
        _load_cast(jobs, layer, stage, sem)

    n_tiles = d_ff // MXU_WIDTH
    bounds = [((n_tiles * i) // FF_SPLIT) * MXU_WIDTH for i in range(FF_SPLIT)] + [d_ff]

    inv_rms = lambda x: lax.rsqrt(jnp.mean(x * x, axis=-1, keepdims=True) + EPS)

    def process(x_ref, p_ref, o_ref):
        x = x_ref[...]
        r = inv_rms(x)
        xg = (x * g_ref[...]).astype(BF16)
        acc = None
        for lo, hi in zip(bounds[:-1], bounds[1:]):
            gate = _dot(xg, wg_ref[:, lo:hi]) * r
            up = _dot(xg, wu_ref[:, lo:hi]) * r
            part = _dot((_silu(gate) * up).astype(BF16), wd_ref[lo:hi, :])
            acc = part if acc is None else acc + part
        h = x + 0.5 * acc
        if do_ple:
            gate = jax.nn.sigmoid(_dot((h * gp_ref[...]).astype(BF16), wpg_ref[...]) * inv_rms(h))
            h = h + gate * _dot(p_ref[...].astype(BF16), wpi_ref[...])
        if final_norm:
            h = _rms(h, gf_ref[...])
        o_ref[...] = h

    @pl.when(i < n_prompt_tiles)
    def _():
        process(xp_ref, pp_ref, op_ref)

    @pl.when(i == n_prompt_tiles)
    def _():
        process(xs_ref, ps_ref, os_ref)


def _row_call(hp, hs, layer, g, wg, wu, wd, ple=None, final_gain=None):
    n_p, d = hp.shape
    n_s = hs.shape[0]
    d_ff = wg.shape[-1]
    assert n_p % ROW_TILE == 0 and d_ff % MXU_WIDTH == 0
    npt = n_p // ROW_TILE
    do_ple = ple is not None
    final_norm = final_gain is not None

    ptile = lambda i: (jnp.minimum(i, npt - 1), 0)
    whole = lambda i: (0, 0)
    lsel3 = lambda i: (layer, 0, 0)

    in_specs = [pl.BlockSpec((ROW_TILE, d), ptile), pl.BlockSpec((n_s, d), whole)]
    args = [hp, hs]
    if do_ple:
        pp, ps, gp, wpg, wpi = ple
        d_p = pp.shape[-1]
        in_specs += [pl.BlockSpec((None, ROW_TILE, d_p), lambda i: (layer, jnp.minimum(i, npt - 1), 0)),
                     pl.BlockSpec((None, n_s, d_p), lsel3)]
        args += [pp, ps]
    in_hbm = pl.BlockSpec(memory_space=pl.ANY)
    in_specs += [pl.BlockSpec((None, 1, d), lsel3), in_hbm, in_hbm, in_hbm]
    args += [g, wg, wu, wd]
    scratch = [pltpu.VMEM((d, d_ff), BF16), pltpu.VMEM((d, d_ff), BF16), pltpu.VMEM((d_ff, d), BF16)]
    if do_ple:
        in_specs += [pl.BlockSpec((None, 1, d), lsel3), in_hbm, in_hbm]
        args += [gp, wpg, wpi]
        scratch += [pltpu.VMEM((d, d), BF16), pltpu.VMEM((d_p, d), BF16)]
    if final_norm:
        in_specs += [pl.BlockSpec((1, d), whole)]
        args += [final_gain]
    scratch += _weight_stage(max(d, d_ff))

    return pl.pallas_call(
        functools.partial(_row_kernel, layer=layer, n_prompt_tiles=npt, d_ff=d_ff, do_ple=do_ple,
                          final_norm=final_norm),
        grid=(npt + 1,),
        in_specs=in_specs,
        out_specs=[pl.BlockSpec((ROW_TILE, d), ptile), pl.BlockSpec((n_s, d), whole)],
        out_shape=[jax.ShapeDtypeStruct(hp.shape, F32), jax.ShapeDtypeStruct(hs.shape, F32)],
        scratch_shapes=scratch,
        compiler_params=pltpu.CompilerParams(dimension_semantics=("arbitrary",), vmem_limit_bytes=VMEM_LIMIT),
        name=f"rows_l{layer}_{'ple' if do_ple else 'ffn'}",
    )(*args)


N_LEVELS = CHUNK.bit_length() - 1


def _score_owner():
    t = lax.broadcasted_iota(jnp.int32, (CHUNK, CHUNK), 0)
    s = lax.broadcasted_iota(jnp.int32, (CHUNK, CHUNK), 1)
    return jnp.where(s > t, -1, jnp.where(s == t, N_LEVELS, 31 - lax.clz(t ^ s)))


def _hgrn_scores(q, k, lf, q_adj, owner):
    nb = CHUNK // SUB
    row = lax.broadcasted_iota(jnp.int32, (SUB, D_HEAD), 0)
    bcast = lambda xb, r: jnp.broadcast_to(xb[r:r + 1, :], (SUB, D_HEAD))

    g_blocks, ends = [], []
    carry = None
    for b in range(nb):
        xb = lf[SUB * b:SUB * (b + 1), :]
        for sft in (1, 2, 4):
            xb = xb + jnp.where(row >= sft, pltpu.roll(xb, sft, axis=0), 0.0)
        if carry is not None:
            xb = xb + carry
        carry = bcast(xb, SUB - 1)
        g_blocks.append(xb)
        ends.append(carry)
    g = jnp.concatenate(g_blocks, axis=0)

    def level_factor(level):
        bs = 1 << level
        if bs == 2:
            return [jnp.exp(-jnp.abs(gb - jnp.where(row < 4, bcast(gb, 1), bcast(gb, 5)))) for gb in g_blocks]
        if bs == 4:
            return [jnp.exp(-jnp.abs(gb - bcast(gb, 3))) for gb in g_blocks]
        n = bs // SUB
        out = []
        for b, gb in enumerate(g_blocks):
            ref = ends[(b // (2 * n)) * (2 * n) + n - 1]
            out.append(jnp.exp(gb - ref if (b // n) % 2 == 1 else ref - gb))
        return out

    a = jnp.where(owner == 0, _dot_nt(q_adj, k.astype(BF16)), 0.0)
    for level in range(1, N_LEVELS):
        e = jnp.concatenate(level_factor(level), axis=0)
        a = jnp.where(owner == level, _dot_nt((q * e).astype(BF16), (k * e).astype(BF16)), a)
    a = jnp.where(owner == N_LEVELS, jnp.sum(q * k, axis=-1, keepdims=True), a)

    g_last = jnp.concatenate([ends[nb - 1]] * nb, axis=0)
    q_dec = q * jnp.exp(g)
    k_dec = k * jnp.exp(g_last - g)
    return a.astype(BF16), q_dec.astype(BF16), k_dec.astype(BF16), jnp.exp(ends[nb - 1])


def _mix_prompt_kernel(h_ref, gm_ref, win_hbm, lb_ref, hn_ref, cw_ref, cb_ref, lng_ref, lnb_ref, wout_hbm,
                       *rest, layer, tt, n_prev):
    (o_ref, s_out_ref, c_out_ref,
     q_s, k_s, lf_s, qa_s, v_s, gs_s, oa_s, yc_s, gbuf, st_s, own_s, a_s, qd_s, kd_s, dec_s,
     win_ref, wout_ref, stage, sem) = rest[n_prev:]
    t = pl.program_id(1)

    @pl.when((pl.program_id(0) == 0) & (t == 0))
    def _():
        _load_cast([(win_hbm, win_ref), (wout_hbm, wout_ref)], layer, stage, sem)

    @pl.when(t == 0)
    def _():
        st_s[...] = jnp.zeros_like(st_s)
        gbuf[:, 0:HIST, :] = jnp.zeros((N_SLAB, HIST, D_HEAD), F32)

    own_s[...] = _score_owner()
    chunk_rows = lambda c: pl.ds(pl.multiple_of(c * CHUNK, CHUNK), CHUNK)

    xn = _rms(h_ref[...], gm_ref[...]).astype(BF16)
    group = lambda j: _dot(xn, win_ref[:, j * D_GROUP:(j + 1) * D_GROUP])
    q = _silu(group(0))
    fr = group(1)
    f, k = _gates(fr, _lower_bound(lb_ref, layer))
    odd = (lax.broadcasted_iota(jnp.int32, (tt, D_GROUP), 0) & 1) == 1
    q_s[...] = q
    qa_s[...] = (q * jnp.where(odd, f, 1.0)).astype(BF16)
    k_s[...] = k
    lf_s[...] = jnp.maximum(jnp.log(f), jnp.minimum(fr, 0.0) - LN2)
    v_s[...] = group(2).astype(BF16)
    gs_s[...] = _silu(group(3))
    glu = group(4) * jax.nn.sigmoid(group(5))
    for j in range(N_SLAB):
        gbuf[j, HIST:HIST + tt, :] = glu[:, j * D_HEAD:(j + 1) * D_HEAD]

    def conv_block(i):
        base = pl.multiple_of(i * CONV_BLOCK, CONV_BLOCK)
        for j in range(N_SLAB):
            js = slice(j * D_HEAD, (j + 1) * D_HEAD)
            accs = [jnp.broadcast_to(cb_ref[:, js], (SUB, D_HEAD)) for _ in range(PHASES)]
            for w in range(CONV_W):
                cw = cw_ref[w:w + 1, js]
                for p in range(PHASES):
                    start = base + (HIST - (CONV_W - 1) + p + w)
                    accs[p] = accs[p] + gbuf[j, pl.ds(start, SUB, stride=PHASES), :] * cw
            for p in range(PHASES):
                yc_s[j, pl.ds(base + p, SUB, stride=PHASES), :] = accs[p]

    def stash_scores(c):
        rows = chunk_rows(c)
        owner = own_s[...]
        for h in range(N_HEADS):
            hs = slice(h * D_HEAD, (h + 1) * D_HEAD)
            a_s[h], qd_s[h], kd_s[h], dec_s[h] = _hgrn_scores(q_s[rows, hs], k_s[rows, hs], lf_s[rows, hs],
                                                               qa_s[rows, hs], owner)

    def advance_state(c):
        rows = chunk_rows(c)
        for h in range(N_HEADS):
            hs = slice(h * D_HEAD, (h + 1) * D_HEAD)
            v = v_s[rows, hs]
            st = st_s[h]
            o = _dot_nt(qd_s[h], st.astype(BF16)) + _dot(a_s[h], v)
            st_s[h] = st * dec_s[h][0:1, :] + _dot_tn(v, kd_s[h])
            oa_s[rows, hs] = _rms(o, hn_ref[:, hs]) * gs_s[rows, hs]

    n_chunks = tt // CHUNK
    conv_per_chunk = tt // CONV_BLOCK // n_chunks

    def chunk_body(c, carry):
        advance_state(c - 1)
        stash_scores(c)
        for r in range(conv_per_chunk):
            conv_block(c * conv_per_chunk + r)
        return carry

    stash_scores(0)
    for r in range(conv_per_chunk):
        conv_block(r)
    lax.fori_loop(1, n_chunks, chunk_body, 0)
    advance_state(n_chunks - 1)

    ob = _layernorm_silu(jnp.concatenate([yc_s[j] for j in range(N_SLAB)], axis=-1), lng_ref[...], lnb_ref[...])
    o_ref[...] = (h_ref[...] + _dot(oa_s[...].astype(BF16), wout_ref[0:D_GROUP, :])
                  + _dot(ob.astype(BF16), wout_ref[D_GROUP:2 * D_GROUP, :]))

    @pl.when(t == pl.num_programs(1) - 1)
    def _():
        lo = tt + HIST - (CONV_W - 1)
        c_out_ref[...] = jnp.concatenate([gbuf[j, lo:lo + CONV_W - 1, :] for j in range(N_SLAB)], axis=-1)
        for h in range(N_HEADS):
            s_out_ref[h] = st_s[h].T

    gbuf[:, 0:HIST, :] = gbuf[:, tt:tt + HIST, :]


def _mixer_weight_specs(layer, depth, d, d_in):
    lsel3 = lambda *_: (layer, 0, 0)
    whole2 = lambda *_: (0, 0)
    in_hbm = pl.BlockSpec(memory_space=pl.ANY)
    return [pl.BlockSpec((None, 1, d), lsel3),
            in_hbm,
            pl.BlockSpec((depth, D_GROUP), whole2),
            pl.BlockSpec((None, 1, D_GROUP), lsel3),
            pl.BlockSpec((None, CONV_W, D_GROUP), lsel3),
            pl.BlockSpec((None, 1, D_GROUP), lsel3),
            pl.BlockSpec((None, 1, D_GROUP), lsel3),
            pl.BlockSpec((None, 1, D_GROUP), lsel3),
            in_hbm]


def _mixer_weight_scratch(d, d_in):
    return [pltpu.VMEM((d, d_in), BF16), pltpu.VMEM((2 * D_GROUP, d), BF16)] + _weight_stage(max(d, d_in))


def _stacked_state_outputs(prev, n_args):
    if prev is None:
        return [], [], {}
    return ([pl.BlockSpec(memory_space=pl.ANY)] * len(prev), list(prev),
            {n_args + i: 1 + i for i in range(len(prev))})


def _mix_prompt_call(hp, layer, batch, seq, mixw, prev):
    n_p, d = hp.shape
    depth, _, d_in = mixw[1].shape
    tt = MIX_TILE
    assert seq % tt == 0 and tt % CHUNK == 0 and tt % CONV_BLOCK == 0 and tt >= HIST
    nt = seq // tt
    tile = lambda b, t: (b * nt + t, 0)
    prev_specs, prev_args, aliases = _stacked_state_outputs(prev, 1 + len(mixw))
    scratch = [pltpu.VMEM((tt, D_GROUP), F32),
               pltpu.VMEM((tt, D_GROUP), F32),
               pltpu.VMEM((tt, D_GROUP), F32),
               pltpu.VMEM((tt, D_GROUP), BF16),
               pltpu.VMEM((tt, D_GROUP), BF16),
               pltpu.VMEM((tt, D_GROUP), F32),
               pltpu.VMEM((tt, D_GROUP), F32),
               pltpu.VMEM((N_SLAB, tt, D_HEAD), F32),
               pltpu.VMEM((N_SLAB, tt + HIST, D_HEAD), F32),
               pltpu.VMEM((N_HEADS, D_HEAD, D_HEAD), F32),
               pltpu.VMEM((CHUNK, CHUNK), jnp.int32),
               pltpu.VMEM((N_HEADS, CHUNK, CHUNK), BF16),
               pltpu.VMEM((N_HEADS, CHUNK, D_HEAD), BF16),
               pltpu.VMEM((N_HEADS, CHUNK, D_HEAD), BF16),
               pltpu.VMEM((N_HEADS, SUB, D_HEAD), F32)]
    scratch += _mixer_weight_scratch(d, d_in)
    return pl.pallas_call(
        functools.partial(_mix_prompt_kernel, layer=layer, tt=tt, n_prev=len(prev_args)),
        grid=(batch, nt),
        in_specs=[pl.BlockSpec((tt, d), tile)] + _mixer_weight_specs(layer, depth, d, d_in) + prev_specs,
        out_specs=[pl.BlockSpec((tt, d), tile),
                   pl.BlockSpec((None, None, N_HEADS, D_HEAD, D_HEAD), lambda b, t: (layer, b, 0, 0, 0)),
                   pl.BlockSpec((None, None, CONV_W - 1, D_GROUP), lambda b, t: (layer, b, 0, 0))],
        out_shape=[jax.ShapeDtypeStruct(hp.shape, F32),
                   jax.ShapeDtypeStruct((depth, batch, N_HEADS, D_HEAD, D_HEAD), F32),
                   jax.ShapeDtypeStruct((depth, batch, CONV_W - 1, D_GROUP), F32)],
        scratch_shapes=scratch,
        input_output_aliases=aliases,
        compiler_params=pltpu.CompilerParams(dimension_semantics=("arbitrary", "arbitrary"),
                                             vmem_limit_bytes=VMEM_LIMIT),
        name=f"mix_prompt_l{layer}",
    )(hp, *mixw, *prev_args)


def _mix_sample_kernel(h_ref, gm_ref, win_hbm, lb_ref, hn_ref, cw_ref, cb_ref, lng_ref, lnb_ref, wout_hbm,
                       s_ref, cs_ref, *rest, layer, n_prev):
    (o_ref, s_out_ref, cs_out_ref,
     q_s, f_s, v_s, g_s, glu_s, oraw_s, yc_s, blk_o,
     win_ref, wout_ref, stage, sem) = rest[n_prev:]
    i = pl.program_id(0)
    sb = SAMPLE_BLOCK

    @pl.when(i == 0)
    def _():
        _load_cast([(win_hbm, win_ref), (wout_hbm, wout_ref)], layer, stage, sem)
        proj =_dot(_rms(h_ref[...], gm_ref[...]).astype(BF16), win_ref[...])
        part = lambda j: proj[:, j * D_GROUP:(j + 1) * D_GROUP]
        f, _ = _gates(part(1), _lower_bound(lb_ref, layer))
        q_s[...] = _silu(part(0))
        f_s[...] = f
        v_s[...] = part(2)
        g_s[...] = _silu(part(3))
        glu_s[...] = part(4) * jax.nn.sigmoid(part(5))

    r0 = pl.multiple_of(i * sb, sb)
    rows = pl.ds(r0, sb)
    f8, q8, v8, glu8 = f_s[rows, :], q_s[rows, :], v_s[rows, :], glu_s[rows, :]
    pad = jnp.zeros((D_HEAD - 2 * sb, D_HEAD), F32)
    for h in range(N_HEADS):
        hs = slice(h * D_HEAD, (h + 1) * D_HEAD)
        cols = jnp.concatenate([f8[:, hs], q8[:, hs], pad], axis=0).T
        for j in range(sb):
            vj = v8[j:j + 1, hs]
            sn = vj + cols[:, j:j + 1] * (s_ref[j, h] - vj)
            s_out_ref[j, h] = sn
            blk_o[j:j + 1, hs] = jnp.sum(sn * cols[:, sb + j:sb + j + 1], axis=0, keepdims=True)
    yc = glu8 * cw_ref[CONV_W - 1:CONV_W, :]
    for w in range(CONV_W - 1):
        yc = yc + cs_ref[w] * cw_ref[w:w + 1, :]
    for w in range(CONV_W - 2):
        cs_out_ref[w] = cs_ref[w + 1]
    cs_out_ref[CONV_W - 2] = glu8
    oraw_s[rows, :] = blk_o[...]
    yc_s[rows, :] = yc

    @pl.when(i == pl.num_programs(0) - 1)
    def _():
        oraw = oraw_s[...]
        oa = jnp.concatenate(
            [_rms(oraw[:, h * D_HEAD:(h + 1) * D_HEAD], hn_ref[:, h * D_HEAD:(h + 1) * D_HEAD])
             for h in range(N_HEADS)], axis=-1) * g_s[...]
        ob = _layernorm_silu(yc_s[...] + cb_ref[...], lng_ref[...], lnb_ref[...])
        o_ref[...] = (h_ref[...] + _dot(oa.astype(BF16), wout_ref[0:D_GROUP, :])
                      + _dot(ob.astype(BF16), wout_ref[D_GROUP:2 * D_GROUP, :]))


def _mix_sample_call(hs, layer, state_hgrn, state_conv, mixw, prev):
    n_s, d = hs.shape
    depth, _, d_in = mixw[1].shape
    sb = SAMPLE_BLOCK
    assert n_s % sb == 0 and 2 * sb <= D_HEAD
    whole = lambda i: (0, 0)
    s_spec = pl.BlockSpec((None, sb, N_HEADS, D_HEAD, D_HEAD), lambda i: (layer, i, 0, 0, 0))
    cs_spec = pl.BlockSpec((None, CONV_W - 1, sb, D_GROUP), lambda i: (layer, 0, i, 0))
    prev_specs, prev_args, aliases = _stacked_state_outputs(prev, 3 + len(mixw))
    scratch = [pltpu.VMEM((n_s, D_GROUP), F32) for _ in range(7)]
    scratch += [pltpu.VMEM((sb, D_GROUP), F32)]
    scratch += _mixer_weight_scratch(d, d_in)
    return pl.pallas_call(
        functools.partial(_mix_sample_kernel, layer=layer, n_prev=len(prev_args)),
        grid=(n_s // sb,),
        in_specs=[pl.BlockSpec((n_s, d), whole)] + _mixer_weight_specs(layer, depth, d, d_in)
        + [s_spec, cs_spec] + prev_specs,
        out_specs=[pl.BlockSpec((n_s, d), whole), s_spec, cs_spec],
        out_shape=[jax.ShapeDtypeStruct(hs.shape, F32),
                   jax.ShapeDtypeStruct(state_hgrn.shape, F32),
                   jax.ShapeDtypeStruct(state_conv.shape, F32)],
        scratch_shapes=scratch,
        input_output_aliases=aliases,
        compiler_params=pltpu.CompilerParams(dimension_semantics=("arbitrary",), vmem_limit_bytes=VMEM_LIMIT),
        name=f"mix_sample_l{layer}",
    )(hs, *mixw, state_hgrn, state_conv, *prev_args)


def kernel(x_prompt, x_sample, state_hgrn, state_conv, p_prompt, p_sample, norm_ffn1, ffn1_w_gate, ffn1_w_up, ffn1_w_down, norm_mix, w_in, hgrn_lb, hgrn_norm, conv_w, conv_b, conv_ln_g, conv_ln_b, w_out, norm_ffn2, ffn2_w_gate, ffn2_w_up, ffn2_w_down, norm_ple, w_ple_gate, w_ple_in, norm_final):
    batch, seq, d = x_prompt.shape
    n_s = x_sample.shape[0] * x_sample.shape[1]
    depth = w_in.shape[0]
    assert w_in.shape[-1] == 6 * D_GROUP and hgrn_lb.shape == (depth, D_GROUP)
    assert state_hgrn.shape[2:] == (N_HEADS, D_HEAD, D_HEAD) and conv_w.shape[1:] == (CONV_W, D_GROUP)

    row3 = lambda a: a.reshape(depth, 1, a.shape[-1])
    ffn1 = (row3(norm_ffn1), ffn1_w_gate, ffn1_w_up, ffn1_w_down)
    ffn2 = (row3(norm_ffn2), ffn2_w_gate, ffn2_w_up, ffn2_w_down)
    mixw = (row3(norm_mix), w_in, hgrn_lb, row3(hgrn_norm), conv_w, row3(conv_b), row3(conv_ln_g),
            row3(conv_ln_b), w_out)
    pp = p_prompt.reshape(depth, batch * seq, p_prompt.shape[-1])
    ps = p_sample.reshape(depth, n_s, p_sample.shape[-1])
    ple = (pp, ps, row3(norm_ple), w_ple_gate, w_ple_in)
    gain_final = norm_final.reshape(1, d)
    conv_taps = jnp.swapaxes(state_conv, 1, 2)

    hp = x_prompt.reshape(batch * seq, d)
    hs = x_sample.reshape(n_s, d)
    new_p = new_s = None
    for layer in range(depth):
        hp, hs = _row_call(hp, hs, layer, *ffn1)
        hp, *new_p = _mix_prompt_call(hp, layer, batch, seq, mixw, new_p)
        hs, *new_s = _mix_sample_call(hs, layer, state_hgrn, conv_taps, mixw, new_s)
        hp, hs = _row_call(hp, hs, layer, *ffn2, ple=ple, final_gain=gain_final if layer == depth - 1 else None)
    return (hp.reshape(x_prompt.shape), hs.reshape(x_sample.shape), new_p[0], new_p[1], new_s[0],
            jnp.swapaxes(new_s[1], 1, 2))
```
